```python
import math
import jax
import jax.numpy as jnp
from jax import lax
import numpy as np

D_MODEL = 2048
BATCH = 1
SEQ = 16384
DEPTH = 2

D_MIX = D_MODEL
D_CONV = D_MIX // 4
D_ATT = D_MIX // 2
D_SSM = D_MIX - D_CONV - D_ATT
CONV_WIDTH = 31
HEAD_DIM = 128
N_HEADS = D_ATT // HEAD_DIM
N_KV_HEADS = 2
GQA = N_HEADS // N_KV_HEADS
KV_W = N_KV_HEADS * HEAD_DIM
ROT_DIM = HEAD_DIM // 4
ROPE_THETA = 500000.0
CMP_LEN = 32
CMP_STRIDE = 16
SLC_LEN = 64
N_SEL = 16
WINDOW = 512
Q_BLOCK = 128
FORCE_BONUS = 1000.0
SSM_GROUP = 16
N_SSM_GROUPS = D_SSM // SSM_GROUP
SSM_STATE = 64
DT_MIN = 1e-3
DT_MAX = 1e-1
D_FF = 5632
N_EXPERTS = 8
TOP_K = 2
D_FF_EXPERT = 7168
N_DENSE = (DEPTH + 1) // 2
N_MOE = DEPTH // 2
EPS = 1e-5

OFF_CONV = 0
OFF_Q = OFF_CONV + 2 * D_CONV
OFF_KV = OFF_Q + D_ATT
OFF_GATE = OFF_KV + 6 * KV_W
OFF_SSM = OFF_GATE + 3 * N_HEADS
D_IN = OFF_SSM + D_SSM

kernel_name = 'hybrid_conv_nsa_s5_moe_block'


def _rmsnorm(x, g):
    xf = x.astype(jnp.float32)
    y = xf * lax.rsqrt(jnp.mean(xf * xf, axis=-1, keepdims=True) + EPS)
    return (y * g.astype(jnp.float32)).astype(x.dtype)


def _layernorm(x, g, b):
    xf = x.astype(jnp.float32)
    mu = jnp.mean(xf, axis=-1, keepdims=True)
    var = jnp.mean(jnp.square(xf - mu), axis=-1, keepdims=True)
    y = (xf - mu) * lax.rsqrt(var + EPS) * g.astype(jnp.float32) + b.astype(jnp.float32)
    return y.astype(x.dtype)


def _masked_softmax(s, mask):
    s = jnp.where(mask, s.astype(jnp.float32), -jnp.inf)
    m = jnp.max(s, axis=-1, keepdims=True)
    m = jnp.where(jnp.isfinite(m), m, 0.0)
    p = jnp.exp(s - m)
    return p / jnp.maximum(jnp.sum(p, axis=-1, keepdims=True), 1e-30)


def _rope_tables(positions):
    inv = ROPE_THETA ** (-jnp.arange(0, ROT_DIM, 2, dtype=jnp.float32) / ROT_DIM)
    ang = positions.astype(jnp.float32)[..., None] * inv
    return jnp.cos(ang)[:, :, None, :], jnp.sin(ang)[:, :, None, :]


def _rope_partial(t, cos, sin):
    half = ROT_DIM // 2
    tf = t[..., :ROT_DIM].astype(jnp.float32)
    t1, t2 = tf[..., :half], tf[..., half:]
    rot = jnp.concatenate([t1 * cos - t2 * sin, t2 * cos + t1 * sin], axis=-1)
    return jnp.concatenate([rot.astype(t.dtype), t[..., ROT_DIM:]], axis=-1)


def _swiglu(x, wg, wu, wd):
    return (jax.nn.silu(x @ wg) * (x @ wu)) @ wd


def _conv_module(h, dw_w, dw_b, ln_g, ln_b, pw_w, pw_b):
    a, b = h[..., :D_CONV], h[..., D_CONV:]
    u = a * jax.nn.sigmoid(b)
    y = lax.conv_general_dilated(
        u, dw_w[:, None, :], window_strides=(1,), padding=[(CONV_WIDTH - 1, 0)],
        dimension_numbers=('NWC', 'WIO', 'NWC'), feature_group_count=D_CONV) + dw_b
    y = jax.nn.silu(_layernorm(y, ln_g, ln_b))
    return y @ pw_w + pw_b


def _nsa_compress(k, pe, w1, w2):
    B, S, G, hd = k.shape
    ch = k.reshape(B, S // CMP_STRIDE, CMP_STRIDE, G, hd)
    blk = jnp.concatenate([ch[:, :-1], ch[:, 1:]], axis=2)
    hid = jax.nn.silu(jnp.einsum('bnlgd,lde->bnge', blk + pe[None, None, :, None, :], w1))
    return jnp.einsum('bnge,ef->bngf', hid, w2)


def _nsa_mixer(q, kc_raw, vc_raw, ks, vs, kw, vw, gates, positions,
               pe_k, pe_v, w1k, w2k, w1v, w2v):
    B, S = q.shape[0], q.shape[1]
    G, R, hd = N_KV_HEADS, GQA, HEAD_DIM
    cos, sin = _rope_tables(positions)
    q_rope = _rope_partial(q, cos, sin)
    ks = _rope_partial(ks, cos, sin)
    kw = _rope_partial(kw, cos, sin)
    kc = _nsa_compress(kc_raw, pe_k, w1k, w2k)
    vc = _nsa_compress(vc_raw, pe_v, w1v, w2v)
    NC = kc.shape[1]
    NS = S // SLC_LEN
    n_sel = min(N_SEL, NS)
    ratio = SLC_LEN // CMP_STRIDE
    cmp_end = jnp.arange(NC) * CMP_STRIDE + CMP_LEN - 1
    ks_blk = ks.reshape(B, NS, SLC_LEN, G, hd).transpose(0, 3, 1, 2, 4)
    vs_blk = vs.reshape(B, NS, SLC_LEN, G, hd).transpose(0, 3, 1, 2, 4)
    kw_pad = jnp.pad(kw, ((0, 0), (WINDOW, 0), (0, 0), (0, 0)))
    vw_pad = jnp.pad(vw, ((0, 0), (WINDOW, 0), (0, 0), (0, 0)))
    qg_raw = q.reshape(B, S, G, R, hd)
    qg_rope = q_rope.reshape(B, S, G, R, hd)
    scale = HEAD_DIM ** -0.5
    bi = jnp.arange(B)[:, None, None, None]
    gi = jnp.arange(G)[None, :, None, None]
    blk_j = jnp.arange(NS)

    def block_fn(qb):
        q0 = qb * Q_BLOCK
        t = q0 + jnp.arange(Q_BLOCK)
        qr = lax.dynamic_slice_in_dim(qg_raw, q0, Q_BLOCK, axis=1)
        qp = lax.dynamic_slice_in_dim(qg_rope, q0, Q_BLOCK, axis=1)
        s_c = jnp.einsum('bqgrd,bngd->bgrqn', qr, kc) * scale
        p_c = _masked_softmax(s_c, cmp_end[None, :] <= t[:, None])
        o_c = jnp.einsum('bgrqn,bngd->bqgrd', p_c.astype(vc.dtype), vc)
        imp = jnp.pad(jnp.sum(p_c, axis=2), ((0, 0), (0, 0), (0, 0), (1, 0)))
        imp = imp.reshape(B, G, Q_BLOCK, NS, ratio)
        imp_slc = jnp.sum(imp, axis=-1) + jnp.pad(imp[..., 1:, 0], ((0, 0), (0, 0), (0, 0), (0, 1)))
        cur = t // SLC_LEN
        allowed = blk_j[None, :] <= cur[:, None]
        forced = (blk_j[None, :] == 0) | (blk_j[None, :] == cur[:, None]) | (blk_j[None, :] == cur[:, None] - 1)
        score = jnp.where(forced, imp_slc + FORCE_BONUS, imp_slc)
        score = jnp.where(allowed, score, -1.0)
        top_v, top_i = lax.top_k(score, n_sel)
        valid = top_v >= 0.0
        k_sel = ks_blk[bi, gi, top_i]
        v_sel = vs_blk[bi, gi, top_i]
        tok = top_i[..., None] * SLC_LEN + jnp.arange(SLC_LEN)
        m_s = valid[..., None] & (tok <= t[None, None, :, None, None])
        s_s = jnp.einsum('bqgrd,bgqnld->bgrqnl', qp, k_sel) * scale
        nl = n_sel * SLC_LEN
        p_s = _masked_softmax(s_s.reshape(B, G, R, Q_BLOCK, nl), m_s.reshape(B, G, 1, Q_BLOCK, nl))
        o_s = jnp.einsum('bgrqm,bgqmd->bqgrd', p_s.astype(v_sel.dtype), v_sel.reshape(B, G, Q_BLOCK, nl, hd))
        kwb = lax.dynamic_slice_in_dim(kw_pad, q0, Q_BLOCK + WINDOW, axis=1)
        vwb = lax.dynamic_slice_in_dim(vw_pad, q0, Q_BLOCK + WINDOW, axis=1)
        src = q0 - WINDOW + jnp.arange(Q_BLOCK + WINDOW)
        diff = t[:, None] - src[None, :]
        m_w = (diff >= 0) & (diff < WINDOW) & (src[None, :] >= 0)
        s_w = jnp.einsum('bqgrd,bkgd->bgrqk', qp, kwb) * scale
        p_w = _masked_softmax(s_w, m_w)
        o_w = jnp.einsum('bgrqk,bkgd->bqgrd', p_w.astype(vwb.dtype), vwb)
        g = lax.dynamic_slice_in_dim(gates, q0, Q_BLOCK, axis=1)
        return g[..., 0:1] * o_c + g[..., 1:2] * o_s + g[..., 2:3] * o_w

    out = lax.map(block_fn, jnp.arange(S // Q_BLOCK))
    return out.transpose(1, 0, 2, 3, 4, 5).reshape(B, S, D_ATT)


def _cplx_combine(e1, e2):
    a1r, a1i, b1r, b1i = e1
    a2r, a2i, b2r, b2i = e2
    return (a2r * a1r - a2i * a1i,
            a2r * a1i + a2i * a1r,
            a2r * b1r - a2i * b1i + b2r,
            a2r * b1i + a2i * b1r + b2i)


def _ssm_mixer(u, lam_re, lam_im, log_dt, b_re, b_im, c_re, c_im, d_skip, glu_w, glu_b):
    B, S = u.shape[0], u.shape[1]
    f32 = jnp.float32
    uf = u.astype(f32)
    ug = uf.reshape(B, S, N_SSM_GROUPS, SSM_GROUP)
    lr = jnp.minimum(lam_re.astype(f32), -1e-4)
    li = lam_im.astype(f32)
    dt = jnp.exp(log_dt.astype(f32))[:, None]
    mag = jnp.exp(lr * dt)
    ar = mag * jnp.cos(li * dt)
    ai = mag * jnp.sin(li * dt)
    den = lr * lr + li * li
    cr = ((ar - 1.0) * lr + ai * li) / den
    ci = (ai * lr - (ar - 1.0) * li) / den
    br, bim = b_re.astype(f32), b_im.astype(f32)
    bbr = cr[..., None] * br - ci[..., None] * bim
    bbi = cr[..., None] * bim + ci[..., None] * br
    xr = jnp.einsum('bsgc,gpc->bsgp', ug, bbr)
    xi = jnp.einsum('bsgc,gpc->bsgp', ug, bbi)
    a_r = jnp.broadcast_to(ar, xr.shape)
    a_i = jnp.broadcast_to(ai, xi.shape)
    _, _, hr, hi = lax.associative_scan(_cplx_combine, (a_r, a_i, xr, xi), axis=1)
    y = (jnp.einsum('bsgp,gcp->bsgc', hr, c_re.astype(f32))
         - jnp.einsum('bsgp,gcp->bsgc', hi, c_im.astype(f32)))
    y = y.reshape(B, S, D_SSM) + d_skip.astype(f32) * uf
    y = jax.nn.gelu(y) * jax.nn.sigmoid(y @ glu_w.astype(f32) + glu_b.astype(f32))
    return y.astype(u.dtype)


def _moe(x, router_w, wg, wu, wd):
    shp = x.shape
    xf = x.reshape(-1, shp[-1])
    logits = (xf @ router_w).astype(jnp.float32)
    top_v, top_i = lax.top_k(logits, TOP_K)
    w = jax.nn.softmax(top_v, axis=-1)
    combine = jnp.sum(jax.nn.one_hot(top_i, N_EXPERTS, dtype=jnp.float32) * w[..., None], axis=1)
    out = jnp.zeros_like(xf)
    for e in range(N_EXPERTS):
        out = out + combine[:, e:e + 1].astype(x.dtype) * _swiglu(xf, wg[e], wu[e], wd[e])
    return out.reshape(shp)


def setup_inputs(seed: int = 0) -> dict:
    key = jax.random.key(seed)
    ks = iter(jax.random.split(key, 64))
    f32 = jnp.float32

    def nrm(shape, scale):
        return jax.random.normal(next(ks), shape, f32) * scale

    def gain(shape):
        return 1.0 + nrm(shape, 0.02)

    P = SSM_STATE
    lam_im = jnp.pi * jnp.arange(P, dtype=f32)
    return {
        'x': nrm((BATCH, SEQ, D_MODEL), 1.0),
        'positions': jnp.broadcast_to(jnp.arange(SEQ, dtype=jnp.int32), (BATCH, SEQ)),
        'norm_mix_g': gain((DEPTH, D_MODEL)),
        'w_in': nrm((DEPTH, D_MODEL, D_IN), D_MODEL ** -0.5),
        'conv_dw_w': nrm((DEPTH, CONV_WIDTH, D_CONV), CONV_WIDTH ** -0.5),
        'conv_dw_b': nrm((DEPTH, D_CONV), 0.02),
        'conv_ln_g': gain((DEPTH, D_CONV)),
        'conv_ln_b': nrm((DEPTH, D_CONV), 0.02),
        'conv_pw_w': nrm((DEPTH, D_CONV, D_CONV), D_CONV ** -0.5),
        'conv_pw_b': nrm((DEPTH, D_CONV), 0.02),
        'nsa_gate_b': nrm((DEPTH, 3 * N_HEADS), 0.02),
        'nsa_pe_k': nrm((DEPTH, CMP_LEN, HEAD_DIM), 0.02),
        'nsa_pe_v': nrm((DEPTH, CMP_LEN, HEAD_DIM), 0.02),
        'nsa_w1k': nrm((DEPTH, CMP_LEN, HEAD_DIM, HEAD_DIM), (CMP_LEN * HEAD_DIM) ** -0.5),
        'nsa_w2k': nrm((DEPTH, HEAD_DIM, HEAD_DIM), HEAD_DIM ** -0.5),
        'nsa_w1v': nrm((DEPTH, CMP_LEN, HEAD_DIM, HEAD_DIM), (CMP_LEN * HEAD_DIM) ** -0.5),
        'nsa_w2v': nrm((DEPTH, HEAD_DIM, HEAD_DIM), HEAD_DIM ** -0.5),
        'ssm_lambda_re': -0.5 + nrm((DEPTH, N_SSM_GROUPS, P), 0.01),
        'ssm_lambda_im': lam_im + nrm((DEPTH, N_SSM_GROUPS, P), 0.01),
        'ssm_log_dt': jax.random.uniform(next(ks), (DEPTH, N_SSM_GROUPS), f32,
                                         math.log(DT_MIN), math.log(DT_MAX)),
        'ssm_b_re': nrm((DEPTH, N_SSM_GROUPS, P, SSM_GROUP), (2 * SSM_GROUP) ** -0.5),
        'ssm_b_im': nrm((DEPTH, N_SSM_GROUPS, P, SSM_GROUP), (2 * SSM_GROUP) ** -0.5),
        'ssm_c_re': nrm((DEPTH, N_SSM_GROUPS, SSM_GROUP, P), P ** -0.5),
        'ssm_c_im': nrm((DEPTH, N_SSM_GROUPS, SSM_GROUP, P), P ** -0.5),
        'ssm_d': nrm((DEPTH, D_SSM), 1.0),
        'ssm_glu_w': nrm((DEPTH, D_SSM, D_SSM), D_SSM ** -0.5),
        'ssm_glu_b': nrm((DEPTH, D_SSM), 0.02),
        'mix_out_g': gain((DEPTH, D_MIX)),
        'w_out': nrm((DEPTH, D_MIX, D_MODEL), D_MIX ** -0.5),
        'norm_ffn_g': gain((DEPTH, D_MODEL)),
        'ffn_w_gate': nrm((N_DENSE, D_MODEL, D_FF), D_MODEL ** -0.5),
        'ffn_w_up': nrm((N_DENSE, D_MODEL, D_FF), D_MODEL ** -0.5),
        'ffn_w_down': nrm((N_DENSE, D_FF, D_MODEL), D_FF ** -0.5),
        'router_w': nrm((N_MOE, D_MODEL, N_EXPERTS), D_MODEL ** -0.5),
        'moe_w_gate': nrm((N_MOE, N_EXPERTS, D_MODEL, D_FF_EXPERT), D_MODEL ** -0.5),
        'moe_w_up': nrm((N_MOE, N_EXPERTS, D_MODEL, D_FF_EXPERT), D_MODEL ** -0.5),
        'moe_w_down': nrm((N_MOE, N_EXPERTS, D_FF_EXPERT, D_MODEL), D_FF_EXPERT ** -0.5),
        'final_norm_g': gain((D_MODEL,)),
    }


def reference(x, positions, norm_mix_g, w_in, conv_dw_w, conv_dw_b, conv_ln_g, conv_ln_b,
              conv_pw_w, conv_pw_b, nsa_gate_b, nsa_pe_k, nsa_pe_v, nsa_w1k, nsa_w2k,
              nsa_w1v, nsa_w2v, ssm_lambda_re, ssm_lambda_im, ssm_log_dt, ssm_b_re, ssm_b_im,
              ssm_c_re, ssm_c_im, ssm_d, ssm_glu_w, ssm_glu_b, mix_out_g, w_out, norm_ffn_g,
              ffn_w_gate, ffn_w_up, ffn_w_down, router_w, moe_w_gate, moe_w_up, moe_w_down,
              final_norm_g):
    B, S = x.shape[0], x.shape[1]
    h = x
    for l in range(DEPTH):
        hn = _rmsnorm(h, norm_mix_g[l])
        proj = hn @ w_in[l]
        conv_o = _conv_module(proj[..., OFF_CONV:OFF_Q], conv_dw_w[l], conv_dw_b[l],
                              conv_ln_g[l], conv_ln_b[l], conv_pw_w[l], conv_pw_b[l])
        q = proj[..., OFF_Q:OFF_KV].reshape(B, S, N_HEADS, HEAD_DIM)
        kv = proj[..., OFF_KV:OFF_GATE].reshape(B, S, 6, N_KV_HEADS, HEAD_DIM)
        gates = jax.nn.sigmoid(proj[..., OFF_GATE:OFF_SSM] + nsa_gate_b[l]).reshape(
            B, S, N_KV_HEADS, GQA, 3)
        att_o = _nsa_mixer(q, kv[:, :, 0], kv[:, :, 1], kv[:, :, 2], kv[:, :, 3],
                           kv[:, :, 4], kv[:, :, 5], gates, positions,
                           nsa_pe_k[l], nsa_pe_v[l], nsa_w1k[l], nsa_w2k[l],
                           nsa_w1v[l], nsa_w2v[l])
        ssm_o = _ssm_mixer(proj[..., OFF_SSM:D_IN], ssm_lambda_re[l], ssm_lambda_im[l],
                           ssm_log_dt[l], ssm_b_re[l], ssm_b_im[l], ssm_c_re[l],
                           ssm_c_im[l], ssm_d[l], ssm_glu_w[l], ssm_glu_b[l])
        g = mix_out_g[l]
        mixed = jnp.concatenate([
            _rmsnorm(conv_o, g[:D_CONV]),
            _rmsnorm(att_o, g[D_CONV:D_CONV + D_ATT]),
            _rmsnorm(ssm_o, g[D_CONV + D_ATT:]),
        ], axis=-1)
        h = h + mixed @ w_out[l]
        hn = _rmsnorm(h, norm_ffn_g[l])
        if l % 2 == 0:
            i = l // 2
            h = h + _swiglu(hn, ffn_w_gate[i], ffn_w_up[i], ffn_w_down[i])
        else:
            i = l // 2
            h = h + _moe(hn, router_w[i], moe_w_gate[i], moe_w_up[i], moe_w_down[i])
    return _rmsnorm(h, final_norm_g)
```

```python
import functools
import math

import jax
import jax.numpy as jnp
from jax import lax
from jax.experimental import pallas as pl
from jax.experimental.pallas import tpu as pltpu

F32 = jnp.float32
BF16 = jnp.bfloat16

D_MODEL = 2048
D_CONV = 512
D_ATT = 1024
D_SSM = 512
CONV_WIDTH = 31
HEAD_DIM = 128
N_HEADS = 8
N_KV_HEADS = 2
GQA = 4
ROT_DIM = 32
ROPE_THETA = 500000.0
CMP_LEN = 32
CMP_STRIDE = 16
SLC_LEN = 64
SLC_SHIFT = 6
N_SEL = 16
WINDOW = 512
FORCE_BONUS = 1000.0
SSM_GROUP = 16
N_SSM_GROUPS = 32
SSM_STATE = 64
N_EXPERTS = 8
EPS = 1e-5

OFF_Q = 2 * D_CONV
OFF_KV = OFF_Q + D_ATT
OFF_GATE = OFF_KV + 6 * N_KV_HEADS * HEAD_DIM
OFF_SSM = OFF_GATE + 3 * N_HEADS
D_MAIN = 4096

LANES = 128
MASK_NEG = -1e30
VMEM_LIMIT = 56 * 1024 * 1024

SSM_N = N_SSM_GROUPS * SSM_STATE


def _cparams(sem, vmem=None):
    return pltpu.CompilerParams(dimension_semantics=sem, vmem_limit_bytes=vmem)


def _sigmoid(x):
    return 1.0 / (1.0 + jnp.exp(-x))


def _silu(x):
    return x * _sigmoid(x)


def _gelu_tanh(x):
    return 0.5 * x * (1.0 + jnp.tanh(math.sqrt(2.0 / math.pi) * (x + 0.044715 * (x * x * x))))


def _rms(x, g):
    return x * lax.rsqrt(jnp.mean(x * x, axis=-1, keepdims=True) + EPS) * g


def _dot(a, b):
    return jnp.dot(a, b, preferred_element_type=F32)


def _dot_nt(a, b):
    return lax.dot_general(a, b, (((1,), (1,)), ((), ())), preferred_element_type=F32)


def _norm_side_kernel(*refs, exact_side, has_add):
    if has_add:
        x_ref, a_ref, g_ref, w_ref, sum_ref, hn_ref, side_ref = refs
        x = x_ref[...] + a_ref[...]
        sum_ref[...] = x
    else:
        x_ref, g_ref, w_ref, hn_ref, side_ref = refs
        x = x_ref[...]
    y = _rms(x, g_ref[...])
    hn_ref[...] = y.astype(BF16)
    if exact_side:
        side_ref[...] = jnp.dot(y, w_ref[...], preferred_element_type=F32,
                                precision=lax.Precision.HIGHEST)
    else:
        side_ref[...] = _dot(y.astype(BF16), w_ref[...].astype(BF16))


def norm_side(x, g, w_side, exact_side, add=None, tm=512):
    S, D = x.shape
    has_add = add is not None
    blk = pl.BlockSpec((tm, D), lambda i: (i, 0))
    in_specs = [blk] + ([blk] if has_add else []) + [pl.BlockSpec((1, D), lambda i: (0, 0)),
                                                     pl.BlockSpec((D, LANES), lambda i: (0, 0))]
    out_specs = ([blk] if has_add else []) + [blk, pl.BlockSpec((tm, LANES), lambda i: (i, 0))]
    out_shape = (([jax.ShapeDtypeStruct((S, D), F32)] if has_add else [])
                 + [jax.ShapeDtypeStruct((S, D), BF16), jax.ShapeDtypeStruct((S, LANES), F32)])
    args = [x] + ([add] if has_add else []) + [g.reshape(1, D), w_side]
    return pl.pallas_call(
        functools.partial(_norm_side_kernel, exact_side=exact_side, has_add=has_add),
        grid=(S // tm,),
        in_specs=in_specs,
        out_specs=out_specs,
        out_shape=out_shape,
        compiler_params=_cparams(("arbitrary",), VMEM_LIMIT),
        name="norm_side",
    )(*args)


def _mm_kernel(a_ref, w_ref, o_ref):
    o_ref[...] = _dot(a_ref[...], w_ref[...].astype(BF16))


def _mm_res_kernel(a_ref, w_ref, r_ref, o_ref):
    o_ref[...] = r_ref[...] + _dot(a_ref[...], w_ref[...].astype(BF16))


def matmul(a, w, res=None, tm=1024, tn=512):
    M, K = a.shape
    N = w.shape[1]
    tm = min(tm, M)
    in_specs = [pl.BlockSpec((tm, K), lambda j, i: (i, 0)),
                pl.BlockSpec((K, tn), lambda j, i: (0, j))]
    args = [a, w]
    kern = _mm_kernel
    if res is not None:
        in_specs.append(pl.BlockSpec((tm, tn), lambda j, i: (i, j)))
        args.append(res)
        kern = _mm_res_kernel
    return pl.pallas_call(
        kern,
        grid=(N // tn, M // tm),
        in_specs=in_specs,
        out_specs=pl.BlockSpec((tm, tn), lambda j, i: (i, j)),
        out_shape=jax.ShapeDtypeStruct((M, N), F32),
        compiler_params=_cparams(("arbitrary", "arbitrary"), VMEM_LIMIT),
        name="matmul",
    )(*args)


CONV_HALO = 32
CONV_RC = 32


def _conv_kernel(a_ref, b_ref, ah_ref, bh_ref, dww_ref, dwb_ref, lng_ref, lnb_ref, pww_ref,
                 pwb_ref, g_ref, o_ref, u_ref, y_ref, *, ts):
    i = pl.program_id(0)
    uh = ah_ref[...] * _sigmoid(bh_ref[...])
    u_ref[0:CONV_HALO, :] = jnp.where(i > 0, uh, 0.0)
    u_ref[CONV_HALO:, :] = a_ref[...] * _sigmoid(b_ref[...])
    lead = CONV_HALO - (CONV_WIDTH - 1)
    for c in range(ts // CONV_RC):
        r0 = c * CONV_RC + lead
        acc = dww_ref[0:1, :] * u_ref[r0:r0 + CONV_RC, :]
        for k in range(1, CONV_WIDTH):
            acc = acc + dww_ref[k:k + 1, :] * u_ref[r0 + k:r0 + k + CONV_RC, :]
        y_ref[c * CONV_RC:(c + 1) * CONV_RC, :] = acc + dwb_ref[...]
    y = y_ref[...]
    mu = jnp.mean(y, axis=-1, keepdims=True)
    yc = y - mu
    var = jnp.mean(yc * yc, axis=-1, keepdims=True)
    z = _silu(yc * lax.rsqrt(var + EPS) * lng_ref[...] + lnb_ref[...])
    o = _dot(z.astype(BF16), pww_ref[...].astype(BF16)) + pwb_ref[...]
    o_ref[...] = _rms(o, g_ref[...]).astype(BF16)


def conv_module(proj, dw_w, dw_b, ln_g, ln_b, pw_w, pw_b, g, ts=512):
    S = proj.shape[0]
    C = D_CONV
    hb = ts // CONV_HALO
    row = lambda v: v.reshape(1, C)
    vec = pl.BlockSpec((1, C), lambda i: (0, 0))
    halo = lambda col: pl.BlockSpec((CONV_HALO, C), lambda i: (jnp.maximum(i * hb - 1, 0), col))
    return pl.pallas_call(
        functools.partial(_conv_kernel, ts=ts),
        grid=(S // ts,),
        in_specs=[pl.BlockSpec((ts, C), lambda i: (i, 0)),
                  pl.BlockSpec((ts, C), lambda i: (i, 1)),
                  halo(0), halo(1),
                  pl.BlockSpec((CONV_WIDTH, C), lambda i: (0, 0)),
                  vec, vec, vec,
                  pl.BlockSpec((C, C), lambda i: (0, 0)),
                  vec, vec],
        out_specs=pl.BlockSpec((ts, C), lambda i: (i, 0)),
        out_shape=jax.ShapeDtypeStruct((S, C), BF16),
        scratch_shapes=[pltpu.VMEM((ts + CONV_HALO, C), F32), pltpu.VMEM((ts, C), F32)],
        compiler_params=_cparams(("arbitrary",), VMEM_LIMIT),
        name="conv_module",
    )(proj, proj, proj, proj, dw_w, row(dw_b), row(ln_g), row(ln_b), pw_w, row(pw_b), row(g))


def _prep_kernel(q_ref, kva_ref, kvb_ref, kvc_ref, gl_ref, gb_ref, pos_ref, inv_ref, sgn_ref,
                 qraw_ref, qrope_ref, cv_ref, ks_ref, vs_ref, kw_ref, vw_ref, gate_ref, *, ts):
    i = pl.program_id(0)
    ang = pos_ref[...].astype(F32) * inv_ref[...]
    cos = jnp.cos(ang)
    sin = jnp.sin(ang) * sgn_ref[...]
    lane = lax.broadcasted_iota(jnp.int32, (ts, LANES), 1)
    half = ROT_DIM // 2

    def rope(x):
        sw = jnp.where(lane < half, pltpu.roll(x, LANES - half, 1), pltpu.roll(x, half, 1))
        return x * cos + sw * sin

    scale = HEAD_DIM ** -0.5
    for h in range(N_HEADS):
        q = q_ref[:, h * HEAD_DIM:(h + 1) * HEAD_DIM]
        qraw_ref[h] = (q * scale).astype(BF16)
        qrope_ref[h] = (rope(q) * scale).astype(BF16)
    key = i * ts + lax.broadcasted_iota(jnp.int32, (ts, LANES), 0)
    bias = jnp.where(lane == ((key >> SLC_SHIFT) & (LANES - 1)), MASK_NEG, 0.0).astype(BF16)
    for g in range(N_KV_HEADS):
        lo = g * HEAD_DIM
        hi = (N_KV_HEADS + g) * HEAD_DIM
        cv_ref[g] = kva_ref[:, lo:lo + HEAD_DIM].astype(BF16)
        cv_ref[N_KV_HEADS + g] = kva_ref[:, hi:hi + HEAD_DIM].astype(BF16)
        ks_ref[g, :, 0:HEAD_DIM] = rope(kvb_ref[:, lo:lo + HEAD_DIM]).astype(BF16)
        ks_ref[g, :, HEAD_DIM:2 * HEAD_DIM] = bias
        vs_ref[g] = kvb_ref[:, hi:hi + HEAD_DIM].astype(BF16)
        kw_ref[g] = rope(kvc_ref[:, lo:lo + HEAD_DIM]).astype(BF16)
        vw_ref[g] = kvc_ref[:, hi:hi + HEAD_DIM].astype(BF16)
    gate_ref[...] = _sigmoid(gl_ref[...] + gb_ref[...])


def attn_prep(proj, gate_logits, gate_b, positions, ts=256):
    S = proj.shape[0]
    G = N_KV_HEADS
    inv = ROPE_THETA ** (-jnp.arange(0, ROT_DIM, 2, dtype=F32) / ROT_DIM)
    inv_full = jnp.concatenate([inv, inv, jnp.zeros((LANES - ROT_DIM,), F32)]).reshape(1, LANES)
    half = ROT_DIM // 2
    sgn = jnp.concatenate([-jnp.ones((half,), F32), jnp.ones((half,), F32),
                           jnp.zeros((LANES - ROT_DIM,), F32)]).reshape(1, LANES)
    gb = jnp.concatenate([gate_b, jnp.zeros((LANES - gate_b.shape[0],), F32)]).reshape(1, LANES)
    kvw = 2 * G * HEAD_DIM
    kv0 = OFF_KV // kvw
    row = pl.BlockSpec((1, LANES), lambda i: (0, 0))
    hm = lambda n, w: pl.BlockSpec((n, ts, w), lambda i: (0, i, 0))
    return pl.pallas_call(
        functools.partial(_prep_kernel, ts=ts),
        grid=(S // ts,),
        in_specs=[pl.BlockSpec((ts, D_ATT), lambda i: (i, OFF_Q // D_ATT)),
                  pl.BlockSpec((ts, kvw), lambda i: (i, kv0)),
                  pl.BlockSpec((ts, kvw), lambda i: (i, kv0 + 1)),
                  pl.BlockSpec((ts, kvw), lambda i: (i, kv0 + 2)),
                  pl.BlockSpec((ts, LANES), lambda i: (i, 0)),
                  row,
                  pl.BlockSpec((ts, 1), lambda i: (i, 0)),
                  row, row],
        out_specs=[hm(N_HEADS, HEAD_DIM), hm(N_HEADS, HEAD_DIM), hm(2 * G, HEAD_DIM),
                   hm(G, 2 * HEAD_DIM), hm(G, HEAD_DIM), hm(G, HEAD_DIM), hm(G, HEAD_DIM),
                   pl.BlockSpec((ts, LANES), lambda i: (i, 0))],
        out_shape=[jax.ShapeDtypeStruct((N_HEADS, S, HEAD_DIM), BF16),
                   jax.ShapeDtypeStruct((N_HEADS, S, HEAD_DIM), BF16),
                   jax.ShapeDtypeStruct((2 * G, S, HEAD_DIM), BF16),
                   jax.ShapeDtypeStruct((G, S, 2 * HEAD_DIM), BF16),
                   jax.ShapeDtypeStruct((G, S, HEAD_DIM), BF16),
                   jax.ShapeDtypeStruct((G, S, HEAD_DIM), BF16),
                   jax.ShapeDtypeStruct((G, S, HEAD_DIM), BF16),
                   jax.ShapeDtypeStruct((S, LANES), F32)],
        compiler_params=_cparams(("arbitrary",), VMEM_LIMIT),
        name="attn_prep",
    )(proj, proj, proj, proj, gate_logits, gb, positions.reshape(S, 1), inv_full, sgn)


def _compress_kernel(x_ref, w1_ref, w2_ref, pe_ref, o_ref, ot_ref, *, nch):
    x = x_ref[0]
    w1a = w1_ref[0, 0].astype(BF16)
    w1b = w1_ref[0, 1].astype(BF16)
    first = _dot(x, w1a)
    second = _dot(x, w1b)
    pe_a = jnp.broadcast_to(pe_ref[0, 0], (8, x.shape[1])).astype(BF16)
    pe_b = jnp.broadcast_to(pe_ref[0, 1], (8, x.shape[1])).astype(BF16)
    c0 = (_dot(pe_a, w1a) + _dot(pe_b, w1b))[0:1, :]
    hid = _silu(first + pltpu.roll(second, nch - 1, 0) + c0)
    out = _dot(hid.astype(BF16), w2_ref[0].astype(BF16))
    rowi = lax.broadcasted_iota(jnp.int32, out.shape, 0)
    out = jnp.where(rowi < nch - 1, out, 0.0)
    o_ref[0] = out.astype(BF16)
    ot_ref[0] = out.T.astype(BF16)


def nsa_compress(cv_raw, w1, w2, pe):
    n4, S, hd = cv_raw.shape
    G = N_KV_HEADS
    nch = S // CMP_STRIDE
    cw = CMP_STRIDE * hd
    x = cv_raw.reshape(n4, nch, cw)
    w1r = w1.reshape(2, 2, cw, hd)
    per = pe.reshape(2, 2, 1, cw)
    return pl.pallas_call(
        functools.partial(_compress_kernel, nch=nch),
        grid=(n4,),
        in_specs=[pl.BlockSpec((1, nch, cw), lambda i: (i, 0, 0)),
                  pl.BlockSpec((1, 2, cw, hd), lambda i: (i // G, 0, 0, 0)),
                  pl.BlockSpec((1, hd, hd), lambda i: (i // G, 0, 0)),
                  pl.BlockSpec((1, 2, 1, cw), lambda i: (i // G, 0, 0, 0))],
        out_specs=[pl.BlockSpec((1, nch, hd), lambda i: (i, 0, 0)),
                   pl.BlockSpec((1, hd, nch), lambda i: (i, 0, 0))],
        out_shape=[jax.ShapeDtypeStruct((n4, nch, hd), BF16),
                   jax.ShapeDtypeStruct((n4, hd, nch), BF16)],
        compiler_params=_cparams(("arbitrary",), VMEM_LIMIT),
        name="nsa_compress",
    )(x, w1r, w2, per)


CMP_PAD = 8


def _cmp_kernel(q_ref, kc_ref, vct_ref, oc_ref, mask_ref, pad_ref, *, tq, nch, ns, n_sel):
    qb = pl.program_id(1)
    q0 = qb * tq
    t = q0 + lax.broadcasted_iota(jnp.int32, (1, tq), 1)
    n_idx = lax.broadcasted_iota(jnp.int32, (nch, tq), 0)
    visible = (n_idx * CMP_STRIDE + (CMP_LEN - 1)) <= t
    kc = kc_ref[0]
    vct = vct_ref[0]
    psum = jnp.zeros((nch, tq), F32)
    for r in range(GQA):
        s = jnp.where(visible, _dot_nt(kc, q_ref[r]), MASK_NEG)
        m = jnp.max(s, axis=0, keepdims=True)
        p = jnp.where(visible, jnp.exp(s - m), 0.0)
        denom = jnp.maximum(jnp.sum(p, axis=0, keepdims=True), 1e-30)
        p = p * (1.0 / denom)
        psum = psum + p
        ot = _dot(vct, p.astype(BF16))
        oc_ref[:, r * HEAD_DIM:(r + 1) * HEAD_DIM] = ot.T
    ratio = SLC_LEN // CMP_STRIDE
    pad_ref[0:CMP_PAD, :] = jnp.zeros((CMP_PAD, tq), F32)
    pad_ref[CMP_PAD:, :] = psum
    imp = pad_ref[pl.ds(CMP_PAD - 1, ns, stride=ratio), :]
    for r in range(ratio):
        imp = imp + pad_ref[pl.ds(CMP_PAD + r, ns, stride=ratio), :]
    j = lax.broadcasted_iota(jnp.int32, (ns, tq), 0)
    cur = t >> SLC_SHIFT
    allowed = j <= cur
    forced = (j == 0) | (j == cur) | (j == cur - 1)
    score = jnp.where(forced, imp + FORCE_BONUS, imp)
    score = jnp.where(allowed, score, -1.0)
    picked = jnp.zeros((ns, tq), F32)
    for _ in range(n_sel):
        mx = jnp.max(score, axis=0, keepdims=True)
        first = jnp.min(jnp.where(score == mx, j, ns), axis=0, keepdims=True)
        hit = j == first
        picked = jnp.where(hit, 1.0, picked)
        score = jnp.where(hit, -jnp.inf, score)
    dropped = jnp.where(allowed, 1.0 - picked, 1.0)
    mask_ref[0] = dropped.T.astype(BF16)


def nsa_compressed(q_raw, kc, vct, tq=128):
    _, S, hd = q_raw.shape
    G = N_KV_HEADS
    nch = S // CMP_STRIDE
    ns = S // SLC_LEN
    n_sel = min(N_SEL, ns)
    return pl.pallas_call(
        functools.partial(_cmp_kernel, tq=tq, nch=nch, ns=ns, n_sel=n_sel),
        grid=(G, S // tq),
        in_specs=[pl.BlockSpec((GQA, tq, hd), lambda g, i: (g, i, 0)),
                  pl.BlockSpec((1, nch, hd), lambda g, i: (g, 0, 0)),
                  pl.BlockSpec((1, hd, nch), lambda g, i: (G + g, 0, 0))],
        out_specs=[pl.BlockSpec((tq, GQA * hd), lambda g, i: (i, g)),
                   pl.BlockSpec((1, tq, ns), lambda g, i: (g, i, 0))],
        out_shape=[jax.ShapeDtypeStruct((S, D_ATT), F32),
                   jax.ShapeDtypeStruct((G, S, ns), BF16)],
        scratch_shapes=[pltpu.VMEM((nch + CMP_PAD, tq), F32)],
        compiler_params=_cparams(("arbitrary", "arbitrary"), VMEM_LIMIT),
        name="nsa_compressed",
    )(q_raw, kc, vct)


SEL_TK = 512


def _sel_kernel(q_ref, mask_ref, k_ref, v_ref, o_ref, qa_ref, m_ref, l_ref, acc_ref,
                *, tq, tk, nparts):
    qb = pl.program_id(1)
    q0 = qb * tq
    rows = GQA * tq
    hd = HEAD_DIM
    nkt = (q0 + tq + tk - 1) // tk
    tiles_per_part = (LANES * SLC_LEN) // tk
    qa_ref[:, 0:hd] = q_ref[...].reshape(rows, hd)
    m_ref[...] = jnp.full((rows, 1), MASK_NEG, F32)
    l_ref[...] = jnp.zeros((rows, 1), F32)
    acc_ref[...] = jnp.zeros((rows, hd), F32)
    t_row = q0 + (lax.broadcasted_iota(jnp.int32, (rows, 1), 0) & (tq - 1))

    def tile(kt, diag):
        k0 = pl.multiple_of(kt * tk, tk)
        s = _dot_nt(qa_ref[...], k_ref[0, pl.ds(k0, tk), :])
        if diag:
            key = k0 + lax.broadcasted_iota(jnp.int32, (1, tk), 1)
            s = jnp.where(key <= t_row, s, MASK_NEG)
        m_old = m_ref[...]
        m_new = jnp.maximum(m_old, jnp.max(s, axis=1, keepdims=True))
        alpha = jnp.exp(m_old - m_new)
        p = jnp.exp(s - m_new)
        l_ref[...] = alpha * l_ref[...] + jnp.sum(p, axis=1, keepdims=True)
        acc_ref[...] = alpha * acc_ref[...] + _dot(p.astype(BF16), v_ref[0, pl.ds(k0, tk), :])
        m_ref[...] = m_new

    for part in range(nparts):
        flags = mask_ref[0, :, part * LANES:(part + 1) * LANES]
        for r in range(GQA):
            qa_ref[r * tq:(r + 1) * tq, hd:hd + LANES] = flags
        lo = part * tiles_per_part
        hi = jnp.minimum((part + 1) * tiles_per_part, nkt - 1)

        def body(kt, carry):
            tile(kt, False)
            return carry

        lax.fori_loop(lo, hi, body, 0)

        @pl.when((nkt - 1) // tiles_per_part == part)
        def _():
            tile(nkt - 1, True)

    o = acc_ref[...] * (1.0 / l_ref[...])
    for r in range(GQA):
        o_ref[:, r * hd:(r + 1) * hd] = o[r * tq:(r + 1) * tq, :]


def nsa_selected(q_rope, mask, ks_aug, vs, tq=128, tk=SEL_TK):
    _, S, hd = q_rope.shape
    G = N_KV_HEADS
    tk = min(tk, S)
    nsp = mask.shape[2]
    nparts = nsp // LANES
    rows = GQA * tq
    return pl.pallas_call(
        functools.partial(_sel_kernel, tq=tq, tk=tk, nparts=nparts),
        grid=(G, S // tq),
        in_specs=[pl.BlockSpec((GQA, tq, hd), lambda g, i: (g, i, 0)),
                  pl.BlockSpec((1, tq, nsp), lambda g, i: (g, i, 0)),
                  pl.BlockSpec((1, S, hd + LANES), lambda g, i: (g, 0, 0)),
                  pl.BlockSpec((1, S, hd), lambda g, i: (g, 0, 0))],
        out_specs=pl.BlockSpec((tq, GQA * hd), lambda g, i: (i, g)),
        out_shape=jax.ShapeDtypeStruct((S, D_ATT), F32),
        scratch_shapes=[pltpu.VMEM((rows, hd + LANES), BF16),
                        pltpu.VMEM((rows, 1), F32),
                        pltpu.VMEM((rows, 1), F32),
                        pltpu.VMEM((rows, hd), F32)],
        compiler_params=_cparams(("arbitrary", "arbitrary"), VMEM_LIMIT),
        name="nsa_selected",
    )(q_rope, mask, ks_aug, vs)


def _win_kernel(q_ref, k_ref, v_ref, o_ref, *, tq, span):
    qb = pl.program_id(1)
    q0 = qb * tq
    rows = GQA * tq
    hd = HEAD_DIM
    start = pl.multiple_of(jnp.maximum(q0 + tq - span, 0), tq)
    q = q_ref[...].reshape(rows, hd)
    s = _dot_nt(q, k_ref[0, pl.ds(start, span), :])
    t_row = q0 + (lax.broadcasted_iota(jnp.int32, (rows, 1), 0) & (tq - 1))
    src = start + lax.broadcasted_iota(jnp.int32, (1, span), 1)
    diff = t_row - src
    ok = (diff >= 0) & (diff < WINDOW)
    s = jnp.where(ok, s, MASK_NEG)
    m = jnp.max(s, axis=1, keepdims=True)
    p = jnp.where(ok, jnp.exp(s - m), 0.0)
    l = jnp.sum(p, axis=1, keepdims=True)
    o = _dot(p.astype(BF16), v_ref[0, pl.ds(start, span), :]) * (1.0 / l)
    for r in range(GQA):
        o_ref[:, r * hd:(r + 1) * hd] = o[r * tq:(r + 1) * tq, :]


def nsa_window(q_rope, kw, vw, tq=128):
    _, S, hd = q_rope.shape
    G = N_KV_HEADS
    span = min(WINDOW + tq, S)
    return pl.pallas_call(
        functools.partial(_win_kernel, tq=tq, span=span),
        grid=(G, S // tq),
        in_specs=[pl.BlockSpec((GQA, tq, hd), lambda g, i: (g, i, 0)),
                  pl.BlockSpec((1, S, hd), lambda g, i: (g, 0, 0)),
                  pl.BlockSpec((1, S, hd), lambda g, i: (g, 0, 0))],
        out_specs=pl.BlockSpec((tq, GQA * hd), lambda g, i: (i, g)),
        out_shape=jax.ShapeDtypeStruct((S, D_ATT), F32),
        compiler_params=_cparams(("arbitrary", "arbitrary"), VMEM_LIMIT),
        name="nsa_window",
    )(q_rope, kw, vw)


def _gate_kernel(oc_ref, os_ref, ow_ref, gate_ref, g_ref, o_ref, acc_ref):
    hd = HEAD_DIM
    for h in range(N_HEADS):
        sl = slice(h * hd, (h + 1) * hd)
        acc_ref[:, sl] = (gate_ref[:, 3 * h:3 * h + 1] * oc_ref[:, sl]
                          + gate_ref[:, 3 * h + 1:3 * h + 2] * os_ref[:, sl]
                          + gate_ref[:, 3 * h + 2:3 * h + 3] * ow_ref[:, sl])
    o_ref[...] = _rms(acc_ref[...], g_ref[...]).astype(BF16)


def nsa_gate(o_c, o_s, o_w, gates, g, ts=512):
    S = o_c.shape[0]
    blk = pl.BlockSpec((ts, D_ATT), lambda i: (i, 0))
    return pl.pallas_call(
        _gate_kernel,
        grid=(S // ts,),
        in_specs=[blk, blk, blk,
                  pl.BlockSpec((ts, LANES), lambda i: (i, 0)),
                  pl.BlockSpec((1, D_ATT), lambda i: (0, 0))],
        out_specs=blk,
        out_shape=jax.ShapeDtypeStruct((S, D_ATT), BF16),
        scratch_shapes=[pltpu.VMEM((ts, D_ATT), F32)],
        compiler_params=_cparams(("arbitrary",), VMEM_LIMIT),
        name="nsa_gate",
    )(o_c, o_s, o_w, gates, g.reshape(1, D_ATT))


def _ssm_kernel(u_ref, wbr_ref, wbi_ref, wcr_ref, wci_ref, stepr_ref, stepi_ref, powr_ref,
                powi_ref, d_ref, gw_ref, gb_ref, g_ref, o_ref, hr_ref, hi_ref, cr_ref, ci_ref,
                *, ts):
    i = pl.program_id(0)

    @pl.when(i == 0)
    def _():
        cr_ref[...] = jnp.zeros_like(cr_ref)
        ci_ref[...] = jnp.zeros_like(ci_ref)

    u = u_ref[...]
    ub = u.astype(BF16)
    hr_ref[...] = _dot(ub, wbr_ref[...])
    hi_ref[...] = _dot(ub, wbi_ref[...])
    rowi = lax.broadcasted_iota(jnp.int32, (ts, 1), 0)
    d = 1
    k = 0
    while d < ts:
        hr = hr_ref[...]
        hi = hi_ref[...]
        live = rowi >= d
        sr = jnp.where(live, pltpu.roll(hr, d, 0), 0.0)
        si = jnp.where(live, pltpu.roll(hi, d, 0), 0.0)
        ar = stepr_ref[k:k + 1, :]
        ai = stepi_ref[k:k + 1, :]
        hr_ref[...] = hr + (ar * sr - ai * si)
        hi_ref[...] = hi + (ar * si + ai * sr)
        d *= 2
        k += 1
    cr = cr_ref[...]
    ci = ci_ref[...]
    pr = powr_ref[...]
    pi = powi_ref[...]
    hr = hr_ref[...] + (pr * cr - pi * ci)
    hi = hi_ref[...] + (pr * ci + pi * cr)
    cr_ref[...] = hr[ts - 1:ts, :]
    ci_ref[...] = hi[ts - 1:ts, :]
    y = _dot(hr.astype(BF16), wcr_ref[...]) - _dot(hi.astype(BF16), wci_ref[...])
    y = y + d_ref[...] * u
    gate = _sigmoid(_dot(y.astype(BF16), gw_ref[...].astype(BF16)) + gb_ref[...])
    o_ref[...] = _rms(_gelu_tanh(y) * gate, g_ref[...]).astype(BF16)


def ssm_mixer(proj, lam_re, lam_im, log_dt, b_re, b_im, c_re, c_im, d_skip, glu_w, glu_b, g,
              ts=256):
    S = proj.shape[0]
    NG, P, GC = N_SSM_GROUPS, SSM_STATE, SSM_GROUP
    lr = jnp.minimum(lam_re, -1e-4)
    li = lam_im
    dt = jnp.exp(log_dt)[:, None]
    mag = jnp.exp(lr * dt)
    ar = mag * jnp.cos(li * dt)
    ai = mag * jnp.sin(li * dt)
    den = lr * lr + li * li
    cr = ((ar - 1.0) * lr + ai * li) / den
    ci = (ai * lr - (ar - 1.0) * li) / den
    bbr = cr[..., None] * b_re - ci[..., None] * b_im
    bbi = cr[..., None] * b_im + ci[..., None] * b_re
    eye = jnp.eye(NG, dtype=F32)
    wb = lambda b: jnp.einsum('gpc,gh->gchp', b, eye).reshape(NG * GC, NG * P).astype(BF16)
    wc = lambda c: jnp.einsum('gcp,gh->gphc', c, eye).reshape(NG * P, NG * GC).astype(BF16)

    def powers(n):
        n = n.astype(F32)[:, None]
        lrd = (lr * dt).reshape(1, NG * P)
        lid = (li * dt).reshape(1, NG * P)
        m = jnp.exp(n * lrd)
        return m * jnp.cos(n * lid), m * jnp.sin(n * lid)

    nsteps = int(math.log2(ts))
    stepr, stepi = powers(2 ** jnp.arange(nsteps))
    powr, powi = powers(jnp.arange(1, ts + 1))
    row = lambda v: v.reshape(1, -1)
    full = lambda a: pl.BlockSpec(a.shape, lambda i: (0,) * a.ndim)
    args = [wb(bbr), wb(bbi), wc(c_re), wc(c_im), stepr, stepi, powr, powi, row(d_skip), glu_w,
            row(glu_b), row(g)]
    return pl.pallas_call(
        functools.partial(_ssm_kernel, ts=ts),
        grid=(S // ts,),
        in_specs=[pl.BlockSpec((ts, D_SSM), lambda i: (i, (D_MAIN - D_SSM) // D_SSM))]
                 + [full(a) for a in args],
        out_specs=pl.BlockSpec((ts, D_SSM), lambda i: (i, 0)),
        out_shape=jax.ShapeDtypeStruct((S, D_SSM), BF16),
        scratch_shapes=[pltpu.VMEM((ts, SSM_N), F32), pltpu.VMEM((ts, SSM_N), F32),
                        pltpu.VMEM((1, SSM_N), F32), pltpu.VMEM((1, SSM_N), F32)],
        compiler_params=_cparams(("arbitrary",), VMEM_LIMIT),
        name="ssm_mixer",
    )(proj, *args)


FFN_TM = 1024
FFN_TF = 256
FFN_SUB = 256


def _swiglu_kernel(te_ref, nv_ref, x_ref, wg_ref, wu_ref, wd_ref, o_ref, *, tm):
    i = pl.program_id(0)
    j = pl.program_id(1)
    nvalid = nv_ref[i]
    wg = wg_ref[0].astype(BF16)
    wu = wu_ref[0].astype(BF16)
    wd = wd_ref[0].astype(BF16)
    for sb in range(tm // FFN_SUB):
        sl = slice(sb * FFN_SUB, (sb + 1) * FFN_SUB)

        @pl.when(sb * FFN_SUB < nvalid)
        def _():
            x = x_ref[sl, :]
            a = (_silu(_dot(x, wg)) * _dot(x, wu)).astype(BF16)
            y = _dot(a, wd)

            @pl.when(j == 0)
            def _():
                o_ref[sl, :] = y

            @pl.when(j > 0)
            def _():
                o_ref[sl, :] += y

        @pl.when((sb * FFN_SUB >= nvalid) & (j == 0))
        def _():
            o_ref[sl, :] = jnp.zeros((FFN_SUB, o_ref.shape[1]), F32)


def swiglu(x, wg, wu, wd, tile_expert, tile_valid, tm=FFN_TM, tf=FFN_TF):
    N, D = x.shape
    F = wg.shape[2]
    tm = min(tm, N)
    last = F // tf - 1

    def wj(i, j, nv):
        return jnp.where(nv[i] > 0, j, last)

    in_specs = [pl.BlockSpec((tm, D), lambda i, j, te, nv: (i, 0)),
                pl.BlockSpec((1, D, tf), lambda i, j, te, nv: (te[i], 0, wj(i, j, nv))),
                pl.BlockSpec((1, D, tf), lambda i, j, te, nv: (te[i], 0, wj(i, j, nv))),
                pl.BlockSpec((1, tf, D), lambda i, j, te, nv: (te[i], wj(i, j, nv), 0))]
    return pl.pallas_call(
        functools.partial(_swiglu_kernel, tm=tm),
        grid_spec=pltpu.PrefetchScalarGridSpec(
            num_scalar_prefetch=2,
            grid=(N // tm, F // tf),
            in_specs=in_specs,
            out_specs=pl.BlockSpec((tm, D), lambda i, j, te, nv: (i, 0))),
        out_shape=jax.ShapeDtypeStruct((N, D), F32),
        compiler_params=_cparams(("arbitrary", "arbitrary"), VMEM_LIMIT),
        name="swiglu",
    )(tile_expert, tile_valid, x, wg, wu, wd)


def _route_kernel(lg_ref, info_ref, cnt_ref, run_ref, *, tb):
    i = pl.program_id(0)

    @pl.when(i == 0)
    def _():
        run_ref[...] = jnp.zeros_like(run_ref)

    lane = lax.broadcasted_iota(jnp.int32, (tb, LANES), 1)
    lg = jnp.where(lane < N_EXPERTS, lg_ref[...], -jnp.inf)
    m1 = jnp.max(lg, axis=1, keepdims=True)
    i1 = jnp.min(jnp.where(lg == m1, lane, LANES), axis=1, keepdims=True)
    lg2 = jnp.where(lane == i1, -jnp.inf, lg)
    m2 = jnp.max(lg2, axis=1, keepdims=True)
    i2 = jnp.min(jnp.where(lg2 == m2, lane, LANES), axis=1, keepdims=True)
    e2 = jnp.exp(m2 - m1)
    inv = 1.0 / (1.0 + e2)
    w1 = inv
    w2 = e2 * inv
    onehot = jnp.where((lane == i1) | (lane == i2), 1.0, 0.0)
    r = lax.broadcasted_iota(jnp.int32, (tb, tb), 0)
    c = lax.broadcasted_iota(jnp.int32, (tb, tb), 1)
    tri = jnp.where(c < r, 1.0, 0.0).astype(BF16)
    rank = _dot(tri, onehot.astype(BF16)) + run_ref[...]
    r1 = jnp.sum(jnp.where(lane == i1, rank, 0.0), axis=1, keepdims=True)
    r2 = jnp.sum(jnp.where(lane == i2, rank, 0.0), axis=1, keepdims=True)
    info = jnp.where(lane == 0, i1.astype(F32), 0.0)
    info = jnp.where(lane == 1, i2.astype(F32), info)
    info = jnp.where(lane == 2, r1, info)
    info = jnp.where(lane == 3, r2, info)
    info = jnp.where(lane == 4, w1, info)
    info = jnp.where(lane == 5, w2, info)
    info_ref[...] = info
    run_ref[...] = run_ref[...] + jnp.sum(onehot, axis=0, keepdims=True)
    cnt_ref[...] = run_ref[...]


def moe_route(logits, tb=512):
    S = logits.shape[0]
    return pl.pallas_call(
        functools.partial(_route_kernel, tb=tb),
        grid=(S // tb,),
        in_specs=[pl.BlockSpec((tb, LANES), lambda i: (i, 0))],
        out_specs=[pl.BlockSpec((tb, LANES), lambda i: (i, 0)),
                   pl.BlockSpec((1, LANES), lambda i: (0, 0))],
        out_shape=[jax.ShapeDtypeStruct((S, LANES), F32), jax.ShapeDtypeStruct((1, LANES), F32)],
        scratch_shapes=[pltpu.VMEM((1, LANES), F32)],
        compiler_params=_cparams(("arbitrary",)),
        name="moe_route",
    )(logits)


DISPATCH_TB = 256


def _dispatch_kernel(pos_ref, x_ref, init_ref, o_ref, sem, *, tb, s):
    del init_ref
    i = pl.program_id(0)

    def copy(r, k):
        dst = pos_ref[k * s + i * tb + r]
        return pltpu.make_async_copy(x_ref.at[pl.ds(r, 1), :], o_ref.at[pl.ds(dst, 1), :], sem)

    def start(r, carry):
        copy(r, 0).start()
        copy(r, 1).start()
        return carry

    lax.fori_loop(0, tb, start, 0)

    def drain(r, carry):
        copy(r, 0).wait()
        copy(r, 1).wait()
        return carry

    lax.fori_loop(0, tb, drain, 0)


def moe_dispatch(x_words, pos, n_rows, tb=DISPATCH_TB):
    S, W = x_words.shape
    init = jnp.zeros((n_rows, W), x_words.dtype)
    return pl.pallas_call(
        functools.partial(_dispatch_kernel, tb=tb, s=S),
        grid_spec=pltpu.PrefetchScalarGridSpec(
            num_scalar_prefetch=1,
            grid=(S // tb,),
            in_specs=[pl.BlockSpec((tb, W), lambda i, pos: (i, 0)),
                      pl.BlockSpec(memory_space=pl.ANY)],
            out_specs=pl.BlockSpec(memory_space=pl.ANY),
            scratch_shapes=[pltpu.SemaphoreType.DMA(())]),
        out_shape=jax.ShapeDtypeStruct((n_rows, W), x_words.dtype),
        input_output_aliases={2: 0},
        compiler_params=_cparams(("arbitrary",)),
        name="moe_dispatch",
    )(pos, x_words, init)


COMBINE_TB = 256


def _combine_kernel(pos_ref, h_ref, info_ref, g_ref, y_ref, o_ref, buf_ref, sem, *, tb, s):
    i = pl.program_id(0)

    def copy(r, k):
        src = pos_ref[k * s + i * tb + r]
        return pltpu.make_async_copy(y_ref.at[pl.ds(src, 1), :], buf_ref.at[k, pl.ds(r, 1), :], sem)

    def start(r, carry):
        copy(r, 0).start()
        copy(r, 1).start()
        return carry

    lax.fori_loop(0, tb, start, 0)

    def drain(r, carry):
        copy(r, 0).wait()
        copy(r, 1).wait()
        return carry

    lax.fori_loop(0, tb, drain, 0)
    w1 = info_ref[:, 4:5]
    w2 = info_ref[:, 5:6]
    first_lower = info_ref[:, 0:1] < info_ref[:, 1:2]
    ya = jnp.where(first_lower, w1 * buf_ref[0], w2 * buf_ref[1])
    yb = jnp.where(first_lower, w2 * buf_ref[1], w1 * buf_ref[0])
    o_ref[...] = _rms(h_ref[...] + (ya + yb), g_ref[...])


def moe_combine(h, info, pos, y_sorted, g, tb=COMBINE_TB):
    S, D = h.shape
    return pl.pallas_call(
        functools.partial(_combine_kernel, tb=tb, s=S),
        grid_spec=pltpu.PrefetchScalarGridSpec(
            num_scalar_prefetch=1,
            grid=(S // tb,),
            in_specs=[pl.BlockSpec((tb, D), lambda i, pos: (i, 0)),
                      pl.BlockSpec((tb, LANES), lambda i, pos: (i, 0)),
                      pl.BlockSpec((1, D), lambda i, pos: (0, 0)),
                      pl.BlockSpec(memory_space=pl.ANY)],
            out_specs=pl.BlockSpec((tb, D), lambda i, pos: (i, 0)),
            scratch_shapes=[pltpu.VMEM((2, tb, D), F32), pltpu.SemaphoreType.DMA(())]),
        out_shape=jax.ShapeDtypeStruct((S, D), F32),
        compiler_params=_cparams(("arbitrary",), VMEM_LIMIT),
        name="moe_combine",
    )(pos, h, info, g.reshape(1, D), y_sorted)


def moe_layer(h, hn, logits, wg, wu, wd, final_g, tm=FFN_TM):
    S, D = h.shape
    E = N_EXPERTS
    tm = min(tm, S)
    info, counts = moe_route(logits)
    cnt = counts[0, :E].astype(jnp.int32)
    padded = ((cnt + tm - 1) // tm) * tm
    ends = jnp.cumsum(padded)
    offs = ends - padded
    n_tiles = (2 * S) // tm + E
    e12 = info[:, 0:2].astype(jnp.int32)
    r12 = info[:, 2:4].astype(jnp.int32)
    pos = (jnp.take(offs, e12) + r12).T.reshape(2 * S)
    tile_start = jnp.arange(n_tiles, dtype=jnp.int32) * tm
    tile_expert = jnp.minimum(jnp.sum(tile_start[:, None] >= ends[None, :], axis=1), E - 1)
    tile_expert = tile_expert.astype(jnp.int32)
    tile_valid = jnp.clip(jnp.take(offs + cnt, tile_expert) - tile_start, 0, tm).astype(jnp.int32)
    x_words = lax.bitcast_convert_type(hn.reshape(S, D // 2, 2), jnp.uint32)
    xs_words = moe_dispatch(x_words, pos, n_tiles * tm)
    xs = lax.bitcast_convert_type(xs_words, BF16).reshape(n_tiles * tm, D)
    ys = swiglu(xs, wg, wu, wd, tile_expert, tile_valid, tm=tm)
    return moe_combine(h, info, pos, ys, final_g)


def _final_norm_kernel(x_ref, g_ref, o_ref):
    o_ref[...] = _rms(x_ref[...], g_ref[...])


def final_norm(x, g, tm=512):
    S, D = x.shape
    return pl.pallas_call(
        _final_norm_kernel,
        grid=(S // tm,),
        in_specs=[pl.BlockSpec((tm, D), lambda i: (i, 0)), pl.BlockSpec((1, D), lambda i: (0, 0))],
        out_specs=pl.BlockSpec((tm, D), lambda i: (i, 0)),
        out_shape=jax.ShapeDtypeStruct((S, D), F32),
        compiler_params=_cparams(("arbitrary",), VMEM_LIMIT),
        name="final_norm",
    )(x, g.reshape(1, D))


def _pad_cols(w, n):
    return jnp.concatenate([w, jnp.zeros((w.shape[0], n - w.shape[1]), w.dtype)], axis=1)


def kernel(x, positions, norm_mix_g, w_in, conv_dw_w, conv_dw_b, conv_ln_g, conv_ln_b, conv_pw_w, conv_pw_b, nsa_gate_b, nsa_pe_k, nsa_pe_v, nsa_w1k, nsa_w2k, nsa_w1v, nsa_w2v, ssm_lambda_re, ssm_lambda_im, ssm_log_dt, ssm_b_re, ssm_b_im, ssm_c_re, ssm_c_im, ssm_d, ssm_glu_w, ssm_glu_b, mix_out_g, w_out, norm_ffn_g, ffn_w_gate, ffn_w_up, ffn_w_down, router_w, moe_w_gate, moe_w_up, moe_w_down, final_norm_g):
    B, S, D = x.shape
    depth = w_in.shape[0]
    outs = []
    for b in range(B):
        h = x[b]
        pos = positions[b]
        pending = None
        out = None
        for l in range(depth):
            w_main = jnp.concatenate([w_in[l][:, :OFF_GATE], w_in[l][:, OFF_SSM:]], axis=1)
            w_gate = _pad_cols(w_in[l][:, OFF_GATE:OFF_SSM], LANES)
            if pending is None:
                hn, gate_logits = norm_side(h, norm_mix_g[l], w_gate, exact_side=False)
            else:
                h, hn, gate_logits = norm_side(h, norm_mix_g[l], w_gate, exact_side=False,
                                               add=pending)
                pending = None
            proj = matmul(hn, w_main)
            g_mix = mix_out_g[l]
            conv_o = conv_module(proj, conv_dw_w[l], conv_dw_b[l], conv_ln_g[l], conv_ln_b[l],
                                 conv_pw_w[l], conv_pw_b[l], g_mix[:D_CONV])
            q_raw, q_rope, cv_raw, ks_aug, vs, kw, vw, gates = attn_prep(
                proj, gate_logits, nsa_gate_b[l], pos)
            kc, kct = nsa_compress(cv_raw, jnp.stack([nsa_w1k[l], nsa_w1v[l]]),
                                   jnp.stack([nsa_w2k[l], nsa_w2v[l]]),
                                   jnp.stack([nsa_pe_k[l], nsa_pe_v[l]]))
            o_c, mask = nsa_compressed(q_raw, kc, kct)
            ns = mask.shape[2]
            if ns % LANES:
                mask = jnp.pad(mask, ((0, 0), (0, 0), (0, LANES - ns % LANES)),
                               constant_values=1.0)
            o_s = nsa_selected(q_rope, mask, ks_aug, vs)
            o_w = nsa_window(q_rope, kw, vw)
            att_o = nsa_gate(o_c, o_s, o_w, gates, g_mix[D_CONV:D_CONV + D_ATT])
            ssm_o = ssm_mixer(proj, ssm_lambda_re[l], ssm_lambda_im[l], ssm_log_dt[l],
                              ssm_b_re[l], ssm_b_im[l], ssm_c_re[l], ssm_c_im[l], ssm_d[l],
                              ssm_glu_w[l], ssm_glu_b[l], g_mix[D_CONV + D_ATT:])
            mixed = jnp.concatenate([conv_o, att_o, ssm_o], axis=1)
            h = matmul(mixed, w_out[l], res=h)
            i = l // 2
            if l % 2 == 0:
                hn, _ = norm_side(h, norm_ffn_g[l], jnp.zeros((D, LANES), F32), exact_side=False)
                n_t = S // min(FFN_TM, S)
                pending = swiglu(hn, ffn_w_gate[i:i + 1], ffn_w_up[i:i + 1], ffn_w_down[i:i + 1],
                                 jnp.zeros((n_t,), jnp.int32),
                                 jnp.full((n_t,), min(FFN_TM, S), jnp.int32))
            else:
                if l != depth - 1:
                    raise NotImplementedError("a MoE layer is only supported as the last layer")
                hn, logits = norm_side(h, norm_ffn_g[l], _pad_cols(router_w[i], LANES),
                                       exact_side=True)
                out = moe_layer(h, hn, logits, moe_w_gate[i], moe_w_up[i], moe_w_down[i],
                                final_norm_g)
        if out is None:
            if pending is not None:
                h = h + pending
            out = final_norm(h, final_norm_g)
        outs.append(out)
    return jnp.stack(outs)
```

```python
import functools
import math

import jax
import jax.numpy as jnp
from jax import lax
from jax.experimental import pallas as pl
from jax.experimental.pallas import tpu as pltpu

F32 = jnp.float32
BF16 = jnp.bfloat16

D_MODEL = 2048
D_CONV = 512
D_ATT = 1024
D_SSM = 512
CONV_WIDTH = 31
HEAD_DIM = 128
N_HEADS = 8
N_KV_HEADS = 2
GQA = 4
ROT_DIM = 32
ROPE_THETA = 500000.0
CMP_LEN = 32
CMP_STRIDE = 16
SLC_LEN = 64
SLC_SHIFT = 6
N_SEL = 16
WINDOW = 512
FORCE_BONUS = 1000.0
SSM_GROUP = 16
N_SSM_GROUPS = 32
SSM_STATE = 64
N_EXPERTS = 8
EPS = 1e-5

OFF_Q = 2 * D_CONV
OFF_KV = OFF_Q + D_ATT
OFF_GATE = OFF_KV + 6 * N_KV_HEADS * HEAD_DIM
OFF_SSM = OFF_GATE + 3 * N_HEADS
D_MAIN = 4096

LANES = 128
MASK_NEG = -1e30
BIAS_NEG = -2e30
LOG2E = math.log2(math.e)
VMEM_LIMIT = 56 * 1024 * 1024

SSM_N = N_SSM_GROUPS * SSM_STATE


def _cparams(sem, vmem=None):
    return pltpu.CompilerParams(dimension_semantics=sem, vmem_limit_bytes=vmem)


def _sigmoid(x):
    return 1.0 / (1.0 + jnp.exp(-x))


def _silu(x):
    return x * _sigmoid(x)


def _gelu_tanh(x):
    return 0.5 * x * (1.0 + jnp.tanh(math.sqrt(2.0 / math.pi) * (x + 0.044715 * (x * x * x))))


def _rms(x, g):
    return x * lax.rsqrt(jnp.mean(x * x, axis=-1, keepdims=True) + EPS) * g


def _dot(a, b):
    return jnp.dot(a, b, preferred_element_type=F32)


def _dot_nt(a, b):
    return lax.dot_general(a, b, (((1,), (1,)), ((), ())), preferred_element_type=F32)


def _norm_side_kernel(*refs, exact_side, has_add, emit_hn):
    refs = list(refs)
    x = refs.pop(0)[...]
    if has_add:
        x = x + refs.pop(0)[...]
    g_ref, w_ref = refs.pop(0), refs.pop(0)
    if has_add:
        refs.pop(0)[...] = x
    y = _rms(x, g_ref[...])
    if emit_hn:
        refs.pop(0)[...] = y.astype(BF16)
    (side_ref,) = refs
    if exact_side:
        side_ref[...] = jnp.dot(y, w_ref[...], preferred_element_type=F32,
                                precision=lax.Precision.HIGHEST)
    else:
        side_ref[...] = _dot(y.astype(BF16), w_ref[...].astype(BF16))


def norm_side(x, g, w_side, exact_side, add=None, emit_hn=True, tm=512):
    S, D = x.shape
    has_add = add is not None
    blk = pl.BlockSpec((tm, D), lambda i: (i, 0))
    in_specs = [blk] + ([blk] if has_add else []) + [pl.BlockSpec((1, D), lambda i: (0, 0)),
                                                     pl.BlockSpec((D, LANES), lambda i: (0, 0))]
    out_specs = (([blk] if has_add else []) + ([blk] if emit_hn else [])
                 + [pl.BlockSpec((tm, LANES), lambda i: (i, 0))])
    out_shape = (([jax.ShapeDtypeStruct((S, D), F32)] if has_add else [])
                 + ([jax.ShapeDtypeStruct((S, D), BF16)] if emit_hn else [])
                 + [jax.ShapeDtypeStruct((S, LANES), F32)])
    args = [x] + ([add] if has_add else []) + [g.reshape(1, D), w_side]
    return pl.pallas_call(
        functools.partial(_norm_side_kernel, exact_side=exact_side, has_add=has_add,
                          emit_hn=emit_hn),
        grid=(S // tm,),
        in_specs=in_specs,
        out_specs=out_specs,
        out_shape=out_shape,
        compiler_params=_cparams(("arbitrary",), VMEM_LIMIT),
        name="norm_side",
    )(*args)


def _mm_kernel(a_ref, w_ref, o_ref):
    o_ref[...] = _dot(a_ref[...], w_ref[...].astype(BF16))


def _mm_res_kernel(a_ref, w_ref, r_ref, o_ref):
    o_ref[...] = r_ref[...] + _dot(a_ref[...], w_ref[...].astype(BF16))


def matmul(a, w, res=None, tm=1024, tn=512):
    M, K = a.shape
    N = w.shape[1]
    tm = min(tm, M)
    in_specs = [pl.BlockSpec((tm, K), lambda j, i: (i, 0)),
                pl.BlockSpec((K, tn), lambda j, i: (0, j))]
    args = [a, w]
    kern = _mm_kernel
    if res is not None:
        in_specs.append(pl.BlockSpec((tm, tn), lambda j, i: (i, j)))
        args.append(res)
        kern = _mm_res_kernel
    return pl.pallas_call(
        kern,
        grid=(N // tn, M // tm),
        in_specs=in_specs,
        out_specs=pl.BlockSpec((tm, tn), lambda j, i: (i, j)),
        out_shape=jax.ShapeDtypeStruct((M, N), F32),
        compiler_params=_cparams(("arbitrary", "arbitrary"), VMEM_LIMIT),
        name="matmul",
    )(*args)


CONV_HALO = 32
CONV_RC = 32


def _conv_kernel(a_ref, b_ref, ah_ref, bh_ref, dww_ref, dwb_ref, lng_ref, lnb_ref, pww_ref,
                 pwb_ref, g_ref, o_ref, u_ref, y_ref, *, ts):
    i = pl.program_id(0)
    uh = ah_ref[...] * _sigmoid(bh_ref[...])
    u_ref[0:CONV_HALO, :] = jnp.where(i > 0, uh, 0.0)
    u_ref[CONV_HALO:, :] = a_ref[...] * _sigmoid(b_ref[...])
    lead = CONV_HALO - (CONV_WIDTH - 1)
    for c in range(ts // CONV_RC):
        r0 = c * CONV_RC + lead
        acc = dww_ref[0:1, :] * u_ref[r0:r0 + CONV_RC, :]
        for k in range(1, CONV_WIDTH):
            acc = acc + dww_ref[k:k + 1, :] * u_ref[r0 + k:r0 + k + CONV_RC, :]
        y_ref[c * CONV_RC:(c + 1) * CONV_RC, :] = acc + dwb_ref[...]
    y = y_ref[...]
    mu = jnp.mean(y, axis=-1, keepdims=True)
    yc = y - mu
    var = jnp.mean(yc * yc, axis=-1, keepdims=True)
    z = _silu(yc * lax.rsqrt(var + EPS) * lng_ref[...] + lnb_ref[...])
    o = _dot(z.astype(BF16), pww_ref[...].astype(BF16)) + pwb_ref[...]
    o_ref[...] = _rms(o, g_ref[...]).astype(BF16)


def conv_module(proj, dw_w, dw_b, ln_g, ln_b, pw_w, pw_b, g, ts=512):
    S = proj.shape[0]
    C = D_CONV
    hb = ts // CONV_HALO
    row = lambda v: v.reshape(1, C)
    vec = pl.BlockSpec((1, C), lambda i: (0, 0))
    halo = lambda col: pl.BlockSpec((CONV_HALO, C), lambda i: (jnp.maximum(i * hb - 1, 0), col))
    return pl.pallas_call(
        functools.partial(_conv_kernel, ts=ts),
        grid=(S // ts,),
        in_specs=[pl.BlockSpec((ts, C), lambda i: (i, 0)),
                  pl.BlockSpec((ts, C), lambda i: (i, 1)),
                  halo(0), halo(1),
                  pl.BlockSpec((CONV_WIDTH, C), lambda i: (0, 0)),
                  vec, vec, vec,
                  pl.BlockSpec((C, C), lambda i: (0, 0)),
                  vec, vec],
        out_specs=pl.BlockSpec((ts, C), lambda i: (i, 0)),
        out_shape=jax.ShapeDtypeStruct((S, C), BF16),
        scratch_shapes=[pltpu.VMEM((ts + CONV_HALO, C), F32), pltpu.VMEM((ts, C), F32)],
        compiler_params=_cparams(("arbitrary",), VMEM_LIMIT),
        name="conv_module",
    )(proj, proj, proj, proj, dw_w, row(dw_b), row(ln_g), row(ln_b), pw_w, row(pw_b), row(g))


def _prep_kernel(q_ref, kva_ref, kvb_ref, kvc_ref, gl_ref, gb_ref, pos_ref, inv_ref, sgn_ref,
                 qraw_ref, qrope_ref, qropet_ref, cv_ref, ks_ref, vst_ref, kw_ref, vw_ref,
                 gate_ref, *, ts):
    i = pl.program_id(0)
    ang = pos_ref[...].astype(F32) * inv_ref[...]
    cos = jnp.cos(ang)
    sin = jnp.sin(ang) * sgn_ref[...]
    lane = lax.broadcasted_iota(jnp.int32, (ts, LANES), 1)
    half = ROT_DIM // 2

    def rope(x):
        sw = jnp.where(lane < half, pltpu.roll(x, LANES - half, 1), pltpu.roll(x, half, 1))
        return x * cos + sw * sin

    scale = HEAD_DIM ** -0.5
    for h in range(N_HEADS):
        q = q_ref[:, h * HEAD_DIM:(h + 1) * HEAD_DIM]
        qraw_ref[h] = (q * scale).astype(BF16)
        qr = rope(q) * scale
        qrope_ref[h] = qr.astype(BF16)
        qropet_ref[h] = (qr * LOG2E).T.astype(BF16)
    key = i * ts + lax.broadcasted_iota(jnp.int32, (ts, LANES), 0)
    bias = jnp.where(lane == ((key >> SLC_SHIFT) & (LANES - 1)), BIAS_NEG, 0.0).astype(BF16)
    for g in range(N_KV_HEADS):
        lo = g * HEAD_DIM
        hi = (N_KV_HEADS + g) * HEAD_DIM
        cv_ref[g] = kva_ref[:, lo:lo + HEAD_DIM].astype(BF16)
        cv_ref[N_KV_HEADS + g] = kva_ref[:, hi:hi + HEAD_DIM].astype(BF16)
        ks_ref[g, :, 0:HEAD_DIM] = rope(kvb_ref[:, lo:lo + HEAD_DIM]).astype(BF16)
        ks_ref[g, :, HEAD_DIM:2 * HEAD_DIM] = bias
        vst_ref[g] = kvb_ref[:, hi:hi + HEAD_DIM].T.astype(BF16)
        kw_ref[g] = rope(kvc_ref[:, lo:lo + HEAD_DIM]).astype(BF16)
        vw_ref[g] = kvc_ref[:, hi:hi + HEAD_DIM].astype(BF16)
    gate_ref[...] = _sigmoid(gl_ref[...] + gb_ref[...])


def attn_prep(proj, gate_logits, gate_b, positions, ts=256):
    S = proj.shape[0]
    G = N_KV_HEADS
    inv = ROPE_THETA ** (-jnp.arange(0, ROT_DIM, 2, dtype=F32) / ROT_DIM)
    inv_full = jnp.concatenate([inv, inv, jnp.zeros((LANES - ROT_DIM,), F32)]).reshape(1, LANES)
    half = ROT_DIM // 2
    sgn = jnp.concatenate([-jnp.ones((half,), F32), jnp.ones((half,), F32),
                           jnp.zeros((LANES - ROT_DIM,), F32)]).reshape(1, LANES)
    gb = jnp.concatenate([gate_b, jnp.zeros((LANES - gate_b.shape[0],), F32)]).reshape(1, LANES)
    kvw = 2 * G * HEAD_DIM
    kv0 = OFF_KV // kvw
    row = pl.BlockSpec((1, LANES), lambda i: (0, 0))
    hm = lambda n, w: pl.BlockSpec((n, ts, w), lambda i: (0, i, 0))
    hmt = lambda n: pl.BlockSpec((n, HEAD_DIM, ts), lambda i: (0, 0, i))
    return pl.pallas_call(
        functools.partial(_prep_kernel, ts=ts),
        grid=(S // ts,),
        in_specs=[pl.BlockSpec((ts, D_ATT), lambda i: (i, OFF_Q // D_ATT)),
                  pl.BlockSpec((ts, kvw), lambda i: (i, kv0)),
                  pl.BlockSpec((ts, kvw), lambda i: (i, kv0 + 1)),
                  pl.BlockSpec((ts, kvw), lambda i: (i, kv0 + 2)),
                  pl.BlockSpec((ts, LANES), lambda i: (i, 0)),
                  row,
                  pl.BlockSpec((ts, 1), lambda i: (i, 0)),
                  row, row],
        out_specs=[hm(N_HEADS, HEAD_DIM), hm(N_HEADS, HEAD_DIM), hmt(N_HEADS), hm(2 * G, HEAD_DIM),
                   hm(G, 2 * HEAD_DIM), hmt(G), hm(G, HEAD_DIM), hm(G, HEAD_DIM),
                   pl.BlockSpec((ts, LANES), lambda i: (i, 0))],
        out_shape=[jax.ShapeDtypeStruct((N_HEADS, S, HEAD_DIM), BF16),
                   jax.ShapeDtypeStruct((N_HEADS, S, HEAD_DIM), BF16),
                   jax.ShapeDtypeStruct((N_HEADS, HEAD_DIM, S), BF16),
                   jax.ShapeDtypeStruct((2 * G, S, HEAD_DIM), BF16),
                   jax.ShapeDtypeStruct((G, S, 2 * HEAD_DIM), BF16),
                   jax.ShapeDtypeStruct((G, HEAD_DIM, S), BF16),
                   jax.ShapeDtypeStruct((G, S, HEAD_DIM), BF16),
                   jax.ShapeDtypeStruct((G, S, HEAD_DIM), BF16),
                   jax.ShapeDtypeStruct((S, LANES), F32)],
        compiler_params=_cparams(("arbitrary",), VMEM_LIMIT),
        name="attn_prep",
    )(proj, proj, proj, proj, gate_logits, gb, positions.reshape(S, 1), inv_full, sgn)


def _compress_kernel(x_ref, w1_ref, w2_ref, pe_ref, o_ref, ot_ref, *, nch):
    x = x_ref[0]
    w1a = w1_ref[0, 0].astype(BF16)
    w1b = w1_ref[0, 1].astype(BF16)
    first = _dot(x, w1a)
    second = _dot(x, w1b)
    pe_a = jnp.broadcast_to(pe_ref[0, 0], (8, x.shape[1])).astype(BF16)
    pe_b = jnp.broadcast_to(pe_ref[0, 1], (8, x.shape[1])).astype(BF16)
    c0 = (_dot(pe_a, w1a) + _dot(pe_b, w1b))[0:1, :]
    hid = _silu(first + pltpu.roll(second, nch - 1, 0) + c0)
    out = _dot(hid.astype(BF16), w2_ref[0].astype(BF16))
    rowi = lax.broadcasted_iota(jnp.int32, out.shape, 0)
    out = jnp.where(rowi < nch - 1, out, 0.0)
    o_ref[0] = out.astype(BF16)
    ot_ref[0] = out.T.astype(BF16)


def nsa_compress(cv_raw, w1, w2, pe):
    n4, S, hd = cv_raw.shape
    G = N_KV_HEADS
    nch = S // CMP_STRIDE
    cw = CMP_STRIDE * hd
    x = cv_raw.reshape(n4, nch, cw)
    w1r = w1.reshape(2, 2, cw, hd)
    per = pe.reshape(2, 2, 1, cw)
    return pl.pallas_call(
        functools.partial(_compress_kernel, nch=nch),
        grid=(n4,),
        in_specs=[pl.BlockSpec((1, nch, cw), lambda i: (i, 0, 0)),
                  pl.BlockSpec((1, 2, cw, hd), lambda i: (i // G, 0, 0, 0)),
                  pl.BlockSpec((1, hd, hd), lambda i: (i // G, 0, 0)),
                  pl.BlockSpec((1, 2, 1, cw), lambda i: (i // G, 0, 0, 0))],
        out_specs=[pl.BlockSpec((1, nch, hd), lambda i: (i, 0, 0)),
                   pl.BlockSpec((1, hd, nch), lambda i: (i, 0, 0))],
        out_shape=[jax.ShapeDtypeStruct((n4, nch, hd), BF16),
                   jax.ShapeDtypeStruct((n4, hd, nch), BF16)],
        compiler_params=_cparams(("arbitrary",), VMEM_LIMIT),
        name="nsa_compress",
    )(x, w1r, w2, per)


CMP_PAD = 8


def _cmp_kernel(q_ref, kc_ref, vct_ref, oc_ref, mask_ref, pad_ref, *, tq, nch, ns, n_sel):
    qb = pl.program_id(1)
    q0 = qb * tq
    t = q0 + lax.broadcasted_iota(jnp.int32, (1, tq), 1)
    n_idx = lax.broadcasted_iota(jnp.int32, (nch, tq), 0)
    visible = (n_idx * CMP_STRIDE + (CMP_LEN - 1)) <= t
    kc = kc_ref[0]
    vct = vct_ref[0]
    psum = jnp.zeros((nch, tq), F32)
    for r in range(GQA):
        s = jnp.where(visible, _dot_nt(kc, q_ref[r]), MASK_NEG)
        m = jnp.max(s, axis=0, keepdims=True)
        p = jnp.where(visible, jnp.exp(s - m), 0.0)
        denom = jnp.maximum(jnp.sum(p, axis=0, keepdims=True), 1e-30)
        p = p * (1.0 / denom)
        psum = psum + p
        ot = _dot(vct, p.astype(BF16))
        oc_ref[:, r * HEAD_DIM:(r + 1) * HEAD_DIM] = ot.T
    ratio = SLC_LEN // CMP_STRIDE
    pad_ref[0:CMP_PAD, :] = jnp.zeros((CMP_PAD, tq), F32)
    pad_ref[CMP_PAD:, :] = psum
    imp = pad_ref[pl.ds(CMP_PAD - 1, ns, stride=ratio), :]
    for r in range(ratio):
        imp = imp + pad_ref[pl.ds(CMP_PAD + r, ns, stride=ratio), :]
    j = lax.broadcasted_iota(jnp.int32, (ns, tq), 0)
    cur = t >> SLC_SHIFT
    allowed = j <= cur
    forced = (j == 0) | (j == cur) | (j == cur - 1)
    score = jnp.where(forced, imp + FORCE_BONUS, imp)
    score = jnp.where(allowed, score, -1.0)
    picked = jnp.zeros((ns, tq), F32)
    for _ in range(n_sel):
        mx = jnp.max(score, axis=0, keepdims=True)
        first = jnp.min(jnp.where(score == mx, j, ns), axis=0, keepdims=True)
        hit = j == first
        picked = jnp.where(hit, 1.0, picked)
        score = jnp.where(hit, -jnp.inf, score)
    dropped = jnp.where(allowed, 1.0 - picked, 1.0)
    mask_ref[0] = dropped.astype(BF16)


def nsa_compressed(q_raw, kc, vct, tq=128):
    _, S, hd = q_raw.shape
    G = N_KV_HEADS
    nch = S // CMP_STRIDE
    ns = S // SLC_LEN
    n_sel = min(N_SEL, ns)
    return pl.pallas_call(
        functools.partial(_cmp_kernel, tq=tq, nch=nch, ns=ns, n_sel=n_sel),
        grid=(G, S // tq),
        in_specs=[pl.BlockSpec((GQA, tq, hd), lambda g, i: (g, i, 0)),
                  pl.BlockSpec((1, nch, hd), lambda g, i: (g, 0, 0)),
                  pl.BlockSpec((1, hd, nch), lambda g, i: (G + g, 0, 0))],
        out_specs=[pl.BlockSpec((tq, GQA * hd), lambda g, i: (i, g)),
                   pl.BlockSpec((1, ns, tq), lambda g, i: (g, 0, i))],
        out_shape=[jax.ShapeDtypeStruct((S, D_ATT), F32),
                   jax.ShapeDtypeStruct((G, ns, S), BF16)],
        scratch_shapes=[pltpu.VMEM((nch + CMP_PAD, tq), F32)],
        compiler_params=_cparams(("arbitrary", "arbitrary"), VMEM_LIMIT),
        name="nsa_compressed",
    )(q_raw, kc, vct)


SEL_TK = 512


def _sel_kernel(qt_ref, mask_ref, k_ref, vt_ref, o_ref, qa_ref, s0_ref, s1_ref, p0_ref, p1_ref,
                m_ref, l_ref, acc_ref, *, tq, tk, nparts):
    qb = pl.program_id(1)
    q0 = qb * tq
    cols = GQA * tq
    hd = HEAD_DIM
    last = (q0 + tq - 1) // tk
    tiles_per_part = (LANES * SLC_LEN) // tk
    for part in range(nparts):
        flags = mask_ref[0, part * LANES:(part + 1) * LANES, :]
        for r in range(GQA):
            qa_ref[part, 0:hd, r * tq:(r + 1) * tq] = qt_ref[r]
            qa_ref[part, hd:hd + LANES, r * tq:(r + 1) * tq] = flags
    m_ref[...] = jnp.full((1, cols), MASK_NEG, F32)
    l_ref[...] = jnp.zeros((1, cols), F32)
    acc_ref[...] = jnp.zeros((hd, cols), F32)
    s_bufs = (s0_ref, s1_ref)
    p_bufs = (p0_ref, p1_ref)
    p1_ref[...] = jnp.zeros((tk, cols), BF16)

    def scores(kt):
        k0 = pl.multiple_of(kt * tk, tk)
        return _dot(k_ref[0, pl.ds(k0, tk), :], qa_ref[kt // tiles_per_part])

    def values(kt, p):
        k0 = pl.multiple_of(kt * tk, tk)
        return _dot(vt_ref[0, :, pl.ds(k0, tk)], p)

    def softmax_step(s_buf, p_buf, pv_prev, keep=None):
        sub = 8
        rows = lambda r: slice(r * sub, (r + 1) * sub)
        tile_rows = lambda r: s_buf[rows(r), :] if keep is None else jnp.where(
            keep[rows(r), :], s_buf[rows(r), :], MASK_NEG)
        mx = tile_rows(0)
        for r in range(1, tk // sub):
            mx = jnp.maximum(mx, tile_rows(r))
        m_old = m_ref[...]
        m_new = jnp.maximum(m_old, jnp.max(mx, axis=0, keepdims=True))
        alpha = jnp.exp2(m_old - m_new)
        m_rows = jnp.broadcast_to(m_new, (sub, cols))
        tot = jnp.zeros((sub, cols), F32)
        for r in range(0, tk // sub, 2):
            pa = jnp.exp2(tile_rows(r) - m_rows)
            pb = jnp.exp2(tile_rows(r + 1) - m_rows)
            tot = tot + (pa + pb)
            p_buf[r * sub:(r + 2) * sub, :] = jnp.concatenate([pa, pb], axis=0).astype(BF16)
        l_ref[...] = alpha * l_ref[...] + jnp.sum(tot, axis=0, keepdims=True)
        acc_ref[...] = alpha * (acc_ref[...] + pv_prev)
        m_ref[...] = m_new

    s0_ref[...] = scores(0)

    def step(kt, cur):
        pv_prev = values(jnp.maximum(kt - 1, 0), p_bufs[1 - cur][...])
        s_bufs[1 - cur][...] = scores(kt + 1)
        softmax_step(s_bufs[cur], p_bufs[cur], pv_prev)

    def pair(i, carry):
        step(2 * i, 0)
        step(2 * i + 1, 1)
        return carry

    lax.fori_loop(0, last // 2, pair, 0)

    def finish(cur):
        pv_prev = values(jnp.maximum(last - 1, 0), p_bufs[1 - cur][...])
        key = last * tk + lax.broadcasted_iota(jnp.int32, (tk, 1), 0)
        t_col = q0 + (lax.broadcasted_iota(jnp.int32, (1, cols), 1) & (tq - 1))
        softmax_step(s_bufs[cur], p_bufs[cur], pv_prev, keep=key <= t_col)
        o = (acc_ref[...] + values(last, p_bufs[cur][...])) * (1.0 / l_ref[...])
        for r in range(GQA):
            o_ref[:, r * hd:(r + 1) * hd] = o[:, r * tq:(r + 1) * tq].T

    @pl.when((last & 1) == 1)
    def _():
        step(last - 1, 0)
        finish(1)

    @pl.when((last & 1) == 0)
    def _():
        finish(0)


def nsa_selected(q_rope_t, mask, ks_aug, vs_t, tq=128, tk=SEL_TK):
    _, hd, S = q_rope_t.shape
    G = N_KV_HEADS
    tk = min(tk, S)
    nsp = mask.shape[1]
    nparts = nsp // LANES
    cols = GQA * tq
    return pl.pallas_call(
        functools.partial(_sel_kernel, tq=tq, tk=tk, nparts=nparts),
        grid=(G, S // tq),
        in_specs=[pl.BlockSpec((GQA, hd, tq), lambda g, i: (g, 0, i)),
                  pl.BlockSpec((1, nsp, tq), lambda g, i: (g, 0, i)),
                  pl.BlockSpec((1, S, hd + LANES), lambda g, i: (g, 0, 0)),
                  pl.BlockSpec((1, hd, S), lambda g, i: (g, 0, 0))],
        out_specs=pl.BlockSpec((tq, GQA * hd), lambda g, i: (i, g)),
        out_shape=jax.ShapeDtypeStruct((S, D_ATT), F32),
        scratch_shapes=[pltpu.VMEM((nparts, hd + LANES, cols), BF16),
                        pltpu.VMEM((tk, cols), F32), pltpu.VMEM((tk, cols), F32),
                        pltpu.VMEM((tk, cols), BF16), pltpu.VMEM((tk, cols), BF16),
                        pltpu.VMEM((1, cols), F32),
                        pltpu.VMEM((1, cols), F32),
                        pltpu.VMEM((hd, cols), F32)],
        compiler_params=_cparams(("arbitrary", "arbitrary"), VMEM_LIMIT),
        name="nsa_selected",
    )(q_rope_t, mask, ks_aug, vs_t)


def _win_kernel(q_ref, k_ref, v_ref, o_ref, *, tq, span):
    qb = pl.program_id(1)
    q0 = qb * tq
    rows = GQA * tq
    hd = HEAD_DIM
    start = pl.multiple_of(jnp.maximum(q0 + tq - span, 0), tq)
    q = q_ref[...].reshape(rows, hd)
    s = _dot_nt(q, k_ref[0, pl.ds(start, span), :])
    t_row = q0 + (lax.broadcasted_iota(jnp.int32, (rows, 1), 0) & (tq - 1))
    src = start + lax.broadcasted_iota(jnp.int32, (1, span), 1)
    diff = t_row - src
    ok = (diff >= 0) & (diff < WINDOW)
    s = jnp.where(ok, s, MASK_NEG)
    m = jnp.max(s, axis=1, keepdims=True)
    p = jnp.where(ok, jnp.exp(s - m), 0.0)
    l = jnp.sum(p, axis=1, keepdims=True)
    o = _dot(p.astype(BF16), v_ref[0, pl.ds(start, span), :]) * (1.0 / l)
    for r in range(GQA):
        o_ref[:, r * hd:(r + 1) * hd] = o[r * tq:(r + 1) * tq, :]


def nsa_window(q_rope, kw, vw, tq=128):
    _, S, hd = q_rope.shape
    G = N_KV_HEADS
    span = min(WINDOW + tq, S)
    return pl.pallas_call(
        functools.partial(_win_kernel, tq=tq, span=span),
        grid=(G, S // tq),
        in_specs=[pl.BlockSpec((GQA, tq, hd), lambda g, i: (g, i, 0)),
                  pl.BlockSpec((1, S, hd), lambda g, i: (g, 0, 0)),
                  pl.BlockSpec((1, S, hd), lambda g, i: (g, 0, 0))],
        out_specs=pl.BlockSpec((tq, GQA * hd), lambda g, i: (i, g)),
        out_shape=jax.ShapeDtypeStruct((S, D_ATT), F32),
        compiler_params=_cparams(("arbitrary", "arbitrary"), VMEM_LIMIT),
        name="nsa_window",
    )(q_rope, kw, vw)


def _gate_kernel(oc_ref, os_ref, ow_ref, gate_ref, g_ref, o_ref, acc_ref):
    hd = HEAD_DIM
    for h in range(N_HEADS):
        sl = slice(h * hd, (h + 1) * hd)
        acc_ref[:, sl] = (gate_ref[:, 3 * h:3 * h + 1] * oc_ref[:, sl]
                          + gate_ref[:, 3 * h + 1:3 * h + 2] * os_ref[:, sl]
                          + gate_ref[:, 3 * h + 2:3 * h + 3] * ow_ref[:, sl])
    o_ref[...] = _rms(acc_ref[...], g_ref[...]).astype(BF16)


def nsa_gate(o_c, o_s, o_w, gates, g, ts=512):
    S = o_c.shape[0]
    blk = pl.BlockSpec((ts, D_ATT), lambda i: (i, 0))
    return pl.pallas_call(
        _gate_kernel,
        grid=(S // ts,),
        in_specs=[blk, blk, blk,
                  pl.BlockSpec((ts, LANES), lambda i: (i, 0)),
                  pl.BlockSpec((1, D_ATT), lambda i: (0, 0))],
        out_specs=blk,
        out_shape=jax.ShapeDtypeStruct((S, D_ATT), BF16),
        scratch_shapes=[pltpu.VMEM((ts, D_ATT), F32)],
        compiler_params=_cparams(("arbitrary",), VMEM_LIMIT),
        name="nsa_gate",
    )(o_c, o_s, o_w, gates, g.reshape(1, D_ATT))


def _ssm_kernel(u_ref, wbr_ref, wbi_ref, wcr_ref, wci_ref, stepr_ref, stepi_ref, powr_ref,
                powi_ref, d_ref, gw_ref, gb_ref, g_ref, o_ref, hr_ref, hi_ref, cr_ref, ci_ref,
                *, ts):
    i = pl.program_id(0)

    @pl.when(i == 0)
    def _():
        cr_ref[...] = jnp.zeros_like(cr_ref)
        ci_ref[...] = jnp.zeros_like(ci_ref)

    u = u_ref[...]
    ub = u.astype(BF16)
    hr_ref[...] = _dot(ub, wbr_ref[...])
    hi_ref[...] = _dot(ub, wbi_ref[...])
    rowi = lax.broadcasted_iota(jnp.int32, (ts, 1), 0)
    d = 1
    k = 0
    while d < ts:
        hr = hr_ref[...]
        hi = hi_ref[...]
        live = rowi >= d
        sr = jnp.where(live, pltpu.roll(hr, d, 0), 0.0)
        si = jnp.where(live, pltpu.roll(hi, d, 0), 0.0)
        ar = stepr_ref[k:k + 1, :]
        ai = stepi_ref[k:k + 1, :]
        hr_ref[...] = hr + (ar * sr - ai * si)
        hi_ref[...] = hi + (ar * si + ai * sr)
        d *= 2
        k += 1
    cr = cr_ref[...]
    ci = ci_ref[...]
    pr = powr_ref[...]
    pi = powi_ref[...]
    hr = hr_ref[...] + (pr * cr - pi * ci)
    hi = hi_ref[...] + (pr * ci + pi * cr)
    cr_ref[...] = hr[ts - 1:ts, :]
    ci_ref[...] = hi[ts - 1:ts, :]
    y = _dot(hr.astype(BF16), wcr_ref[...]) - _dot(hi.astype(BF16), wci_ref[...])
    y = y + d_ref[...] * u
    gate = _sigmoid(_dot(y.astype(BF16), gw_ref[...].astype(BF16)) + gb_ref[...])
    o_ref[...] = _rms(_gelu_tanh(y) * gate, g_ref[...]).astype(BF16)


def ssm_mixer(proj, lam_re, lam_im, log_dt, b_re, b_im, c_re, c_im, d_skip, glu_w, glu_b, g,
              ts=256):
    S = proj.shape[0]
    NG, P, GC = N_SSM_GROUPS, SSM_STATE, SSM_GROUP
    lr = jnp.minimum(lam_re, -1e-4)
    li = lam_im
    dt = jnp.exp(log_dt)[:, None]
    mag = jnp.exp(lr * dt)
    ar = mag * jnp.cos(li * dt)
    ai = mag * jnp.sin(li * dt)
    den = lr * lr + li * li
    cr = ((ar - 1.0) * lr + ai * li) / den
    ci = (ai * lr - (ar - 1.0) * li) / den
    bbr = cr[..., None] * b_re - ci[..., None] * b_im
    bbi = cr[..., None] * b_im + ci[..., None] * b_re
    eye = jnp.eye(NG, dtype=F32)
    wb = lambda b: jnp.einsum('gpc,gh->gchp', b, eye).reshape(NG * GC, NG * P).astype(BF16)
    wc = lambda c: jnp.einsum('gcp,gh->gphc', c, eye).reshape(NG * P, NG * GC).astype(BF16)

    def powers(n):
        n = n.astype(F32)[:, None]
        lrd = (lr * dt).reshape(1, NG * P)
        lid = (li * dt).reshape(1, NG * P)
        m = jnp.exp(n * lrd)
        return m * jnp.cos(n * lid), m * jnp.sin(n * lid)

    nsteps = int(math.log2(ts))
    stepr, stepi = powers(2 ** jnp.arange(nsteps))
    powr, powi = powers(jnp.arange(1, ts + 1))
    row = lambda v: v.reshape(1, -1)
    full = lambda a: pl.BlockSpec(a.shape, lambda i: (0,) * a.ndim)
    args = [wb(bbr), wb(bbi), wc(c_re), wc(c_im), stepr, stepi, powr, powi, row(d_skip), glu_w,
            row(glu_b), row(g)]
    return pl.pallas_call(
        functools.partial(_ssm_kernel, ts=ts),
        grid=(S // ts,),
        in_specs=[pl.BlockSpec((ts, D_SSM), lambda i: (i, (D_MAIN - D_SSM) // D_SSM))]
                 + [full(a) for a in args],
        out_specs=pl.BlockSpec((ts, D_SSM), lambda i: (i, 0)),
        out_shape=jax.ShapeDtypeStruct((S, D_SSM), BF16),
        scratch_shapes=[pltpu.VMEM((ts, SSM_N), F32), pltpu.VMEM((ts, SSM_N), F32),
                        pltpu.VMEM((1, SSM_N), F32), pltpu.VMEM((1, SSM_N), F32)],
        compiler_params=_cparams(("arbitrary",), VMEM_LIMIT),
        name="ssm_mixer",
    )(proj, *args)


FFN_TM = 1024
FFN_TF = 256
FFN_SUB = 256


def _swiglu_kernel(te_ref, nv_ref, x_ref, g_ref, wg_ref, wu_ref, wd_ref, o_ref, xn_ref, *, tm):
    i = pl.program_id(0)
    j = pl.program_id(1)
    nvalid = nv_ref[i]
    subs = [slice(sb * FFN_SUB, (sb + 1) * FFN_SUB) for sb in range(tm // FFN_SUB)]

    def weights():
        return wg_ref[0].astype(BF16), wu_ref[0].astype(BF16), wd_ref[0].astype(BF16)

    def normalise(sl):
        xn_ref[sl, :] = _rms(x_ref[sl, :], g_ref[...]).astype(BF16)

    def expert(sl, w):
        x = xn_ref[sl, :]
        a = (_silu(_dot(x, w[0])) * _dot(x, w[1])).astype(BF16)
        return _dot(a, w[2])

    @pl.when((nvalid == tm) & (j == 0))
    def _():
        w = weights()
        for sl in subs:
            normalise(sl)
            o_ref[sl, :] = expert(sl, w)

    @pl.when((nvalid == tm) & (j > 0))
    def _():
        w = weights()
        for sl in subs:
            o_ref[sl, :] += expert(sl, w)

    @pl.when(nvalid < tm)
    def _():
        w = weights()
        for sb, sl in enumerate(subs):
            @pl.when((sb * FFN_SUB < nvalid) & (j == 0))
            def _():
                normalise(sl)
                o_ref[sl, :] = expert(sl, w)

            @pl.when((sb * FFN_SUB < nvalid) & (j > 0))
            def _():
                o_ref[sl, :] += expert(sl, w)

            @pl.when((sb * FFN_SUB >= nvalid) & (j == 0))
            def _():
                o_ref[sl, :] = jnp.zeros((FFN_SUB, o_ref.shape[1]), F32)


def swiglu(x, g, wg, wu, wd, tile_expert, tile_valid, tm=FFN_TM, tf=FFN_TF):
    N, D = x.shape
    F = wg.shape[2]
    tm = min(tm, N)
    last = F // tf - 1

    def wj(i, j, nv):
        return jnp.where(nv[i] > 0, j, last)

    in_specs = [pl.BlockSpec((tm, D), lambda i, j, te, nv: (i, 0)),
                pl.BlockSpec((1, D), lambda i, j, te, nv: (0, 0)),
                pl.BlockSpec((1, D, tf), lambda i, j, te, nv: (te[i], 0, wj(i, j, nv))),
                pl.BlockSpec((1, D, tf), lambda i, j, te, nv: (te[i], 0, wj(i, j, nv))),
                pl.BlockSpec((1, tf, D), lambda i, j, te, nv: (te[i], wj(i, j, nv), 0))]
    return pl.pallas_call(
        functools.partial(_swiglu_kernel, tm=tm),
        grid_spec=pltpu.PrefetchScalarGridSpec(
            num_scalar_prefetch=2,
            grid=(N // tm, F // tf),
            in_specs=in_specs,
            out_specs=pl.BlockSpec((tm, D), lambda i, j, te, nv: (i, 0)),
            scratch_shapes=[pltpu.VMEM((tm, D), BF16)]),
        out_shape=jax.ShapeDtypeStruct((N, D), F32),
        compiler_params=_cparams(("arbitrary", "arbitrary"), VMEM_LIMIT),
        name="swiglu",
    )(tile_expert, tile_valid, x, g.reshape(1, D), wg, wu, wd)


def _route_kernel(lg_ref, info_ref, cnt_ref, run_ref, *, tb):
    i = pl.program_id(0)

    @pl.when(i == 0)
    def _():
        run_ref[...] = jnp.zeros_like(run_ref)

    lane = lax.broadcasted_iota(jnp.int32, (tb, LANES), 1)
    lg = jnp.where(lane < N_EXPERTS, lg_ref[...], -jnp.inf)
    m1 = jnp.max(lg, axis=1, keepdims=True)
    i1 = jnp.min(jnp.where(lg == m1, lane, LANES), axis=1, keepdims=True)
    lg2 = jnp.where(lane == i1, -jnp.inf, lg)
    m2 = jnp.max(lg2, axis=1, keepdims=True)
    i2 = jnp.min(jnp.where(lg2 == m2, lane, LANES), axis=1, keepdims=True)
    e2 = jnp.exp(m2 - m1)
    inv = 1.0 / (1.0 + e2)
    w1 = inv
    w2 = e2 * inv
    onehot = jnp.where((lane == i1) | (lane == i2), 1.0, 0.0)
    r = lax.broadcasted_iota(jnp.int32, (tb, tb), 0)
    c = lax.broadcasted_iota(jnp.int32, (tb, tb), 1)
    tri = jnp.where(c < r, 1.0, 0.0).astype(BF16)
    rank = _dot(tri, onehot.astype(BF16)) + run_ref[...]
    r1 = jnp.sum(jnp.where(lane == i1, rank, 0.0), axis=1, keepdims=True)
    r2 = jnp.sum(jnp.where(lane == i2, rank, 0.0), axis=1, keepdims=True)
    info = jnp.where(lane == 0, i1.astype(F32), 0.0)
    info = jnp.where(lane == 1, i2.astype(F32), info)
    info = jnp.where(lane == 2, r1, info)
    info = jnp.where(lane == 3, r2, info)
    info = jnp.where(lane == 4, w1, info)
    info = jnp.where(lane == 5, w2, info)
    info_ref[...] = info
    run_ref[...] = run_ref[...] + jnp.sum(onehot, axis=0, keepdims=True)
    cnt_ref[...] = run_ref[...]


def moe_route(logits, tb=512):
    S = logits.shape[0]
    return pl.pallas_call(
        functools.partial(_route_kernel, tb=tb),
        grid=(S // tb,),
        in_specs=[pl.BlockSpec((tb, LANES), lambda i: (i, 0))],
        out_specs=[pl.BlockSpec((tb, LANES), lambda i: (i, 0)),
                   pl.BlockSpec((1, LANES), lambda i: (0, 0))],
        out_shape=[jax.ShapeDtypeStruct((S, LANES), F32), jax.ShapeDtypeStruct((1, LANES), F32)],
        scratch_shapes=[pltpu.VMEM((1, LANES), F32)],
        compiler_params=_cparams(("arbitrary",)),
        name="moe_route",
    )(logits)


DISPATCH_TB = 256
SUBLANES = 8
ZERO_ROWS = 128
ZERO_REGIONS = N_EXPERTS + 1


def _dispatch_kernel(pos_ref, fill_ref, x_ref, o_ref, zero_ref, sem, zsem, *, tb, s):
    i = pl.program_id(0)

    def zero_region(first, n, wait):
        def piece(off, rows, aligned):
            if aligned:
                off = pl.multiple_of(off, SUBLANES)
            cp = pltpu.make_async_copy(zero_ref.at[pl.ds(0, rows), :],
                                       o_ref.at[pl.ds(off, rows), :], zsem)
            cp.wait() if wait else cp.start()

        head = jnp.minimum((-first) & (SUBLANES - 1), n)
        lax.fori_loop(0, head, lambda r, c: (piece(first + r, 1, False), c)[1], 0)
        base = first + head
        m = n - head
        big = ZERO_ROWS
        lax.fori_loop(0, m // big, lambda r, c: (piece(base + r * big, big, True), c)[1], 0)
        rem = m % big
        size = big // 2
        while size >= SUBLANES:
            @pl.when((rem & size) != 0)
            def _():
                piece(base + (m - rem) + (rem & ~(2 * size - 1)), size, True)
            size //= 2
        tail0 = base + (m & ~(SUBLANES - 1))
        lax.fori_loop(0, m & (SUBLANES - 1), lambda r, c: (piece(tail0 + r, 1, False), c)[1], 0)

    @pl.when(i == 0)
    def _():
        zero_ref[...] = jnp.zeros_like(zero_ref)
        for wait in (False, True):
            for g in range(ZERO_REGIONS):
                zero_region(fill_ref[g], fill_ref[ZERO_REGIONS + g], wait)

    def copy(r, k):
        dst = pos_ref[k * s + i * tb + r]
        return pltpu.make_async_copy(x_ref.at[pl.ds(r, 1), :], o_ref.at[pl.ds(dst, 1), :], sem)

    def start(r, carry):
        copy(r, 0).start()
        copy(r, 1).start()
        return carry

    lax.fori_loop(0, tb, start, 0)

    def drain(r, carry):
        copy(r, 0).wait()
        copy(r, 1).wait()
        return carry

    lax.fori_loop(0, tb, drain, 0)


def moe_dispatch(x, pos, fill, n_rows, tb=DISPATCH_TB):
    S, D = x.shape
    return pl.pallas_call(
        functools.partial(_dispatch_kernel, tb=tb, s=S),
        grid_spec=pltpu.PrefetchScalarGridSpec(
            num_scalar_prefetch=2,
            grid=(S // tb,),
            in_specs=[pl.BlockSpec((tb, D), lambda i, pos, fill: (i, 0))],
            out_specs=pl.BlockSpec(memory_space=pl.ANY),
            scratch_shapes=[pltpu.VMEM((ZERO_ROWS, D), x.dtype),
                            pltpu.SemaphoreType.DMA(()),
                            pltpu.SemaphoreType.DMA(())]),
        out_shape=jax.ShapeDtypeStruct((n_rows, D), x.dtype),
        compiler_params=_cparams(("arbitrary",)),
        name="moe_dispatch",
    )(pos, fill, x)


COMBINE_TB = 256


def _combine_kernel(pos_ref, h_ref, info_ref, g_ref, y_ref, o_ref, buf_ref, sem, *, tb, s):
    i = pl.program_id(0)

    def copy(r, k):
        src = pos_ref[k * s + i * tb + r]
        return pltpu.make_async_copy(y_ref.at[pl.ds(src, 1), :], buf_ref.at[k, pl.ds(r, 1), :], sem)

    def start(r, carry):
        copy(r, 0).start()
        copy(r, 1).start()
        return carry

    lax.fori_loop(0, tb, start, 0)

    def drain(r, carry):
        copy(r, 0).wait()
        copy(r, 1).wait()
        return carry

    lax.fori_loop(0, tb, drain, 0)
    w1 = info_ref[:, 4:5]
    w2 = info_ref[:, 5:6]
    first_lower = info_ref[:, 0:1] < info_ref[:, 1:2]
    ya = jnp.where(first_lower, w1 * buf_ref[0], w2 * buf_ref[1])
    yb = jnp.where(first_lower, w2 * buf_ref[1], w1 * buf_ref[0])
    o_ref[...] = _rms(h_ref[...] + (ya + yb), g_ref[...])


def moe_combine(h, info, pos, y_sorted, g, tb=COMBINE_TB):
    S, D = h.shape
    return pl.pallas_call(
        functools.partial(_combine_kernel, tb=tb, s=S),
        grid_spec=pltpu.PrefetchScalarGridSpec(
            num_scalar_prefetch=1,
            grid=(S // tb,),
            in_specs=[pl.BlockSpec((tb, D), lambda i, pos: (i, 0)),
                      pl.BlockSpec((tb, LANES), lambda i, pos: (i, 0)),
                      pl.BlockSpec((1, D), lambda i, pos: (0, 0)),
                      pl.BlockSpec(memory_space=pl.ANY)],
            out_specs=pl.BlockSpec((tb, D), lambda i, pos: (i, 0)),
            scratch_shapes=[pltpu.VMEM((2, tb, D), F32), pltpu.SemaphoreType.DMA(())]),
        out_shape=jax.ShapeDtypeStruct((S, D), F32),
        compiler_params=_cparams(("arbitrary",), VMEM_LIMIT),
        name="moe_combine",
    )(pos, h, info, g.reshape(1, D), y_sorted)


def moe_layer(h, norm_g, logits, wg, wu, wd, final_g, tm=FFN_TM):
    S, D = h.shape
    E = N_EXPERTS
    tm = min(tm, S)
    info, counts = moe_route(logits)
    cnt = counts[0, :E].astype(jnp.int32)
    padded = ((cnt + tm - 1) // tm) * tm
    ends = jnp.cumsum(padded)
    offs = ends - padded
    n_tiles = (2 * S) // tm + E
    e12 = info[:, 0:2].astype(jnp.int32)
    r12 = info[:, 2:4].astype(jnp.int32)
    pos = (jnp.take(offs, e12) + r12).T.reshape(2 * S)
    tile_start = jnp.arange(n_tiles, dtype=jnp.int32) * tm
    tile_expert = jnp.minimum(jnp.sum(tile_start[:, None] >= ends[None, :], axis=1), E - 1)
    tile_expert = tile_expert.astype(jnp.int32)
    tile_valid = jnp.clip(jnp.take(offs + cnt, tile_expert) - tile_start, 0, tm).astype(jnp.int32)
    n_rows = n_tiles * tm
    fill = jnp.concatenate([offs + cnt, ends[-1:], padded - cnt, n_rows - ends[-1:]])
    fill = fill.astype(jnp.int32)
    xs = moe_dispatch(h, pos, fill, n_tiles * tm)
    ys = swiglu(xs, norm_g, wg, wu, wd, tile_expert, tile_valid, tm=tm)
    return moe_combine(h, info, pos, ys, final_g)


def _final_norm_kernel(x_ref, g_ref, o_ref):
    o_ref[...] = _rms(x_ref[...], g_ref[...])


def final_norm(x, g, tm=512):
    S, D = x.shape
    return pl.pallas_call(
        _final_norm_kernel,
        grid=(S // tm,),
        in_specs=[pl.BlockSpec((tm, D), lambda i: (i, 0)), pl.BlockSpec((1, D), lambda i: (0, 0))],
        out_specs=pl.BlockSpec((tm, D), lambda i: (i, 0)),
        out_shape=jax.ShapeDtypeStruct((S, D), F32),
        compiler_params=_cparams(("arbitrary",), VMEM_LIMIT),
        name="final_norm",
    )(x, g.reshape(1, D))


def _pad_cols(w, n):
    return jnp.concatenate([w, jnp.zeros((w.shape[0], n - w.shape[1]), w.dtype)], axis=1)


def kernel(x, positions, norm_mix_g, w_in, conv_dw_w, conv_dw_b, conv_ln_g, conv_ln_b, conv_pw_w, conv_pw_b, nsa_gate_b, nsa_pe_k, nsa_pe_v, nsa_w1k, nsa_w2k, nsa_w1v, nsa_w2v, ssm_lambda_re, ssm_lambda_im, ssm_log_dt, ssm_b_re, ssm_b_im, ssm_c_re, ssm_c_im, ssm_d, ssm_glu_w, ssm_glu_b, mix_out_g, w_out, norm_ffn_g, ffn_w_gate, ffn_w_up, ffn_w_down, router_w, moe_w_gate, moe_w_up, moe_w_down, final_norm_g):
    B, S, D = x.shape
    depth = w_in.shape[0]
    outs = []
    for b in range(B):
        h = x[b]
        pos = positions[b]
        pending = None
        out = None
        for l in range(depth):
            w_main = jnp.concatenate([w_in[l][:, :OFF_GATE], w_in[l][:, OFF_SSM:]], axis=1)
            w_gate = _pad_cols(w_in[l][:, OFF_GATE:OFF_SSM], LANES)
            if pending is None:
                hn, gate_logits = norm_side(h, norm_mix_g[l], w_gate, exact_side=False)
            else:
                h, hn, gate_logits = norm_side(h, norm_mix_g[l], w_gate, exact_side=False,
                                               add=pending)
                pending = None
            proj = matmul(hn, w_main)
            g_mix = mix_out_g[l]
            conv_o = conv_module(proj, conv_dw_w[l], conv_dw_b[l], conv_ln_g[l], conv_ln_b[l],
                                 conv_pw_w[l], conv_pw_b[l], g_mix[:D_CONV])
            q_raw, q_rope, q_rope_t, cv_raw, ks_aug, vs_t, kw, vw, gates = attn_prep(
                proj, gate_logits, nsa_gate_b[l], pos)
            kc, kct = nsa_compress(cv_raw, jnp.stack([nsa_w1k[l], nsa_w1v[l]]),
                                   jnp.stack([nsa_w2k[l], nsa_w2v[l]]),
                                   jnp.stack([nsa_pe_k[l], nsa_pe_v[l]]))
            o_c, mask = nsa_compressed(q_raw, kc, kct)
            ns = mask.shape[1]
            if ns % LANES:
                mask = jnp.pad(mask, ((0, 0), (0, LANES - ns % LANES), (0, 0)),
                               constant_values=1.0)
            o_s = nsa_selected(q_rope_t, mask, ks_aug, vs_t)
            o_w = nsa_window(q_rope, kw, vw)
            att_o = nsa_gate(o_c, o_s, o_w, gates, g_mix[D_CONV:D_CONV + D_ATT])
            ssm_o = ssm_mixer(proj, ssm_lambda_re[l], ssm_lambda_im[l], ssm_log_dt[l],
                              ssm_b_re[l], ssm_b_im[l], ssm_c_re[l], ssm_c_im[l], ssm_d[l],
                              ssm_glu_w[l], ssm_glu_b[l], g_mix[D_CONV + D_ATT:])
            mixed = jnp.concatenate([conv_o, att_o, ssm_o], axis=1)
            h = matmul(mixed, w_out[l], res=h)
            i = l // 2
            if l % 2 == 0:
                n_t = S // min(FFN_TM, S)
                pending = swiglu(h, norm_ffn_g[l], ffn_w_gate[i:i + 1], ffn_w_up[i:i + 1],
                                 ffn_w_down[i:i + 1], jnp.zeros((n_t,), jnp.int32),
                                 jnp.full((n_t,), min(FFN_TM, S), jnp.int32))
            else:
                if l != depth - 1:
                    raise NotImplementedError("a MoE layer is only supported as the last layer")
                (logits,) = norm_side(h, norm_ffn_g[l], _pad_cols(router_w[i], LANES),
                                      exact_side=True, emit_hn=False)
                out = moe_layer(h, norm_ffn_g[l], logits, moe_w_gate[i], moe_w_up[i],
                                moe_w_down[i], final_norm_g)
        if out is None:
            if pending is not None:
                h = h + pending
            out = final_norm(h, final_norm_g)
        outs.append(out)
    return jnp.stack(outs)
```

```python
import functools
import math

import jax
import jax.numpy as jnp
from jax import lax
from jax.experimental import pallas as pl
from jax.experimental.pallas import tpu as pltpu

F32 = jnp.float32
BF16 = jnp.bfloat16

D_MODEL = 2048
D_CONV = 512
D_ATT = 1024
D_SSM = 512
CONV_WIDTH = 31
HEAD_DIM = 128
N_HEADS = 8
N_KV_HEADS = 2
GQA = 4
ROT_DIM = 32
ROPE_THETA = 500000.0
CMP_LEN = 32
CMP_STRIDE = 16
SLC_LEN = 64
SLC_SHIFT = 6
N_SEL = 16
WINDOW = 512
FORCE_BONUS = 1000.0
SSM_GROUP = 16
N_SSM_GROUPS = 32
SSM_STATE = 64
N_EXPERTS = 8
EPS = 1e-5

OFF_Q = 2 * D_CONV
OFF_KV = OFF_Q + D_ATT
OFF_GATE = OFF_KV + 6 * N_KV_HEADS * HEAD_DIM
OFF_SSM = OFF_GATE + 3 * N_HEADS
D_MAIN = 4096

LANES = 128
MASK_NEG = -1e30
BIAS_NEG = -2e30
LOG2E = math.log2(math.e)
VMEM_LIMIT = 56 * 1024 * 1024

SSM_N = N_SSM_GROUPS * SSM_STATE


def _cparams(sem, vmem=None):
    return pltpu.CompilerParams(dimension_semantics=sem, vmem_limit_bytes=vmem)


def _sigmoid(x):
    return 1.0 / (1.0 + jnp.exp(-x))


def _silu(x):
    return x * _sigmoid(x)


def _gelu_tanh(x):
    return 0.5 * x * (1.0 + jnp.tanh(math.sqrt(2.0 / math.pi) * (x + 0.044715 * (x * x * x))))


def _rms(x, g):
    return x * lax.rsqrt(jnp.mean(x * x, axis=-1, keepdims=True) + EPS) * g


def _dot(a, b):
    return jnp.dot(a, b, preferred_element_type=F32)


def _dot_nt(a, b):
    return lax.dot_general(a, b, (((1,), (1,)), ((), ())), preferred_element_type=F32)


def _norm_side_kernel(*refs, exact_side, has_add, emit_hn):
    refs = list(refs)
    x = refs.pop(0)[...]
    if has_add:
        x = x + refs.pop(0)[...]
    g_ref, w_ref = refs.pop(0), refs.pop(0)
    if has_add:
        refs.pop(0)[...] = x
    y = _rms(x, g_ref[...])
    if emit_hn:
        refs.pop(0)[...] = y.astype(BF16)
    (side_ref,) = refs
    if exact_side:
        side_ref[...] = jnp.dot(y, w_ref[...], preferred_element_type=F32,
                                precision=lax.Precision.HIGHEST)
    else:
        side_ref[...] = _dot(y.astype(BF16), w_ref[...].astype(BF16))


def norm_side(x, g, w_side, exact_side, add=None, emit_hn=True, tm=512):
    S, D = x.shape
    has_add = add is not None
    blk = pl.BlockSpec((tm, D), lambda i: (i, 0))
    in_specs = [blk] + ([blk] if has_add else []) + [pl.BlockSpec((1, D), lambda i: (0, 0)),
                                                     pl.BlockSpec((D, LANES), lambda i: (0, 0))]
    out_specs = (([blk] if has_add else []) + ([blk] if emit_hn else [])
                 + [pl.BlockSpec((tm, LANES), lambda i: (i, 0))])
    out_shape = (([jax.ShapeDtypeStruct((S, D), F32)] if has_add else [])
                 + ([jax.ShapeDtypeStruct((S, D), BF16)] if emit_hn else [])
                 + [jax.ShapeDtypeStruct((S, LANES), F32)])
    args = [x] + ([add] if has_add else []) + [g.reshape(1, D), w_side]
    return pl.pallas_call(
        functools.partial(_norm_side_kernel, exact_side=exact_side, has_add=has_add,
                          emit_hn=emit_hn),
        grid=(S // tm,),
        in_specs=in_specs,
        out_specs=out_specs,
        out_shape=out_shape,
        compiler_params=_cparams(("arbitrary",), VMEM_LIMIT),
        name="norm_side",
    )(*args)


def _mm_kernel(a_ref, w_ref, o_ref):
    o_ref[...] = _dot(a_ref[...], w_ref[...].astype(BF16))


def _mm_res_kernel(a_ref, w_ref, r_ref, o_ref):
    o_ref[...] = r_ref[...] + _dot(a_ref[...], w_ref[...].astype(BF16))


def matmul(a, w, res=None, tm=1024, tn=512):
    M, K = a.shape
    N = w.shape[1]
    tm = min(tm, M)
    in_specs = [pl.BlockSpec((tm, K), lambda j, i: (i, 0)),
                pl.BlockSpec((K, tn), lambda j, i: (0, j))]
    args = [a, w]
    kern = _mm_kernel
    if res is not None:
        in_specs.append(pl.BlockSpec((tm, tn), lambda j, i: (i, j)))
        args.append(res)
        kern = _mm_res_kernel
    return pl.pallas_call(
        kern,
        grid=(N // tn, M // tm),
        in_specs=in_specs,
        out_specs=pl.BlockSpec((tm, tn), lambda j, i: (i, j)),
        out_shape=jax.ShapeDtypeStruct((M, N), F32),
        compiler_params=_cparams(("arbitrary", "arbitrary"), VMEM_LIMIT),
        name="matmul",
    )(*args)


CONV_HALO = 32
CONV_RC = 32


def _conv_kernel(a_ref, b_ref, ah_ref, bh_ref, dww_ref, dwb_ref, lng_ref, lnb_ref, pww_ref,
                 pwb_ref, g_ref, o_ref, u_ref, y_ref, *, ts):
    i = pl.program_id(0)
    uh = ah_ref[...] * _sigmoid(bh_ref[...])
    u_ref[0:CONV_HALO, :] = jnp.where(i > 0, uh, 0.0)
    u_ref[CONV_HALO:, :] = a_ref[...] * _sigmoid(b_ref[...])
    lead = CONV_HALO - (CONV_WIDTH - 1)
    for c in range(ts // CONV_RC):
        r0 = c * CONV_RC + lead
        acc = dww_ref[0:1, :] * u_ref[r0:r0 + CONV_RC, :]
        for k in range(1, CONV_WIDTH):
            acc = acc + dww_ref[k:k + 1, :] * u_ref[r0 + k:r0 + k + CONV_RC, :]
        y_ref[c * CONV_RC:(c + 1) * CONV_RC, :] = acc + dwb_ref[...]
    y = y_ref[...]
    mu = jnp.mean(y, axis=-1, keepdims=True)
    yc = y - mu
    var = jnp.mean(yc * yc, axis=-1, keepdims=True)
    z = _silu(yc * lax.rsqrt(var + EPS) * lng_ref[...] + lnb_ref[...])
    o = _dot(z.astype(BF16), pww_ref[...].astype(BF16)) + pwb_ref[...]
    o_ref[...] = _rms(o, g_ref[...]).astype(BF16)


def conv_module(proj, dw_w, dw_b, ln_g, ln_b, pw_w, pw_b, g, ts=512):
    S = proj.shape[0]
    C = D_CONV
    hb = ts // CONV_HALO
    row = lambda v: v.reshape(1, C)
    vec = pl.BlockSpec((1, C), lambda i: (0, 0))
    halo = lambda col: pl.BlockSpec((CONV_HALO, C), lambda i: (jnp.maximum(i * hb - 1, 0), col))
    return pl.pallas_call(
        functools.partial(_conv_kernel, ts=ts),
        grid=(S // ts,),
        in_specs=[pl.BlockSpec((ts, C), lambda i: (i, 0)),
                  pl.BlockSpec((ts, C), lambda i: (i, 1)),
                  halo(0), halo(1),
                  pl.BlockSpec((CONV_WIDTH, C), lambda i: (0, 0)),
                  vec, vec, vec,
                  pl.BlockSpec((C, C), lambda i: (0, 0)),
                  vec, vec],
        out_specs=pl.BlockSpec((ts, C), lambda i: (i, 0)),
        out_shape=jax.ShapeDtypeStruct((S, C), BF16),
        scratch_shapes=[pltpu.VMEM((ts + CONV_HALO, C), F32), pltpu.VMEM((ts, C), F32)],
        compiler_params=_cparams(("arbitrary",), VMEM_LIMIT),
        name="conv_module",
    )(proj, proj, proj, proj, dw_w, row(dw_b), row(ln_g), row(ln_b), pw_w, row(pw_b), row(g))


def _prep_kernel(q_ref, kva_ref, kvb_ref, kvc_ref, gl_ref, gb_ref, pos_ref, inv_ref, sgn_ref,
                 qraw_ref, qropet_ref, cv_ref, ks_ref, vst_ref, kw_ref, vwt_ref,
                 gate_ref, *, ts):
    i = pl.program_id(0)
    ang = pos_ref[...].astype(F32) * inv_ref[...]
    cos = jnp.cos(ang)
    sin = jnp.sin(ang) * sgn_ref[...]
    lane = lax.broadcasted_iota(jnp.int32, (ts, LANES), 1)
    half = ROT_DIM // 2

    def rope(x):
        sw = jnp.where(lane < half, pltpu.roll(x, LANES - half, 1), pltpu.roll(x, half, 1))
        return x * cos + sw * sin

    scale = HEAD_DIM ** -0.5
    for h in range(N_HEADS):
        q = q_ref[:, h * HEAD_DIM:(h + 1) * HEAD_DIM]
        qraw_ref[h] = (q * scale).astype(BF16)
        qropet_ref[h] = (rope(q) * (scale * LOG2E)).T.astype(BF16)
    key = i * ts + lax.broadcasted_iota(jnp.int32, (ts, LANES), 0)
    bias = jnp.where(lane == ((key >> SLC_SHIFT) & (LANES - 1)), BIAS_NEG, 0.0).astype(BF16)
    for g in range(N_KV_HEADS):
        lo = g * HEAD_DIM
        hi = (N_KV_HEADS + g) * HEAD_DIM
        cv_ref[g] = kva_ref[:, lo:lo + HEAD_DIM].astype(BF16)
        cv_ref[N_KV_HEADS + g] = kva_ref[:, hi:hi + HEAD_DIM].astype(BF16)
        ks_ref[g, :, 0:HEAD_DIM] = rope(kvb_ref[:, lo:lo + HEAD_DIM]).astype(BF16)
        ks_ref[g, :, HEAD_DIM:2 * HEAD_DIM] = bias
        vst_ref[g] = kvb_ref[:, hi:hi + HEAD_DIM].T.astype(BF16)
        kw_ref[g] = rope(kvc_ref[:, lo:lo + HEAD_DIM]).astype(BF16)
        vwt_ref[g] = kvc_ref[:, hi:hi + HEAD_DIM].T.astype(BF16)
    gate_ref[...] = _sigmoid(gl_ref[...] + gb_ref[...])


def attn_prep(proj, gate_logits, gate_b, positions, ts=256):
    S = proj.shape[0]
    G = N_KV_HEADS
    inv = ROPE_THETA ** (-jnp.arange(0, ROT_DIM, 2, dtype=F32) / ROT_DIM)
    inv_full = jnp.concatenate([inv, inv, jnp.zeros((LANES - ROT_DIM,), F32)]).reshape(1, LANES)
    half = ROT_DIM // 2
    sgn = jnp.concatenate([-jnp.ones((half,), F32), jnp.ones((half,), F32),
                           jnp.zeros((LANES - ROT_DIM,), F32)]).reshape(1, LANES)
    gb = jnp.concatenate([gate_b, jnp.zeros((LANES - gate_b.shape[0],), F32)]).reshape(1, LANES)
    kvw = 2 * G * HEAD_DIM
    kv0 = OFF_KV // kvw
    row = pl.BlockSpec((1, LANES), lambda i: (0, 0))
    hm = lambda n, w: pl.BlockSpec((n, ts, w), lambda i: (0, i, 0))
    hmt = lambda n: pl.BlockSpec((n, HEAD_DIM, ts), lambda i: (0, 0, i))
    return pl.pallas_call(
        functools.partial(_prep_kernel, ts=ts),
        grid=(S // ts,),
        in_specs=[pl.BlockSpec((ts, D_ATT), lambda i: (i, OFF_Q // D_ATT)),
                  pl.BlockSpec((ts, kvw), lambda i: (i, kv0)),
                  pl.BlockSpec((ts, kvw), lambda i: (i, kv0 + 1)),
                  pl.BlockSpec((ts, kvw), lambda i: (i, kv0 + 2)),
                  pl.BlockSpec((ts, LANES), lambda i: (i, 0)),
                  row,
                  pl.BlockSpec((ts, 1), lambda i: (i, 0)),
                  row, row],
        out_specs=[hm(N_HEADS, HEAD_DIM), hmt(N_HEADS), hm(2 * G, HEAD_DIM),
                   hm(G, 2 * HEAD_DIM), hmt(G), hm(G, HEAD_DIM), hmt(G),
                   pl.BlockSpec((ts, LANES), lambda i: (i, 0))],
        out_shape=[jax.ShapeDtypeStruct((N_HEADS, S, HEAD_DIM), BF16),
                   jax.ShapeDtypeStruct((N_HEADS, HEAD_DIM, S), BF16),
                   jax.ShapeDtypeStruct((2 * G, S, HEAD_DIM), BF16),
                   jax.ShapeDtypeStruct((G, S, 2 * HEAD_DIM), BF16),
                   jax.ShapeDtypeStruct((G, HEAD_DIM, S), BF16),
                   jax.ShapeDtypeStruct((G, S, HEAD_DIM), BF16),
                   jax.ShapeDtypeStruct((G, HEAD_DIM, S), BF16),
                   jax.ShapeDtypeStruct((S, LANES), F32)],
        compiler_params=_cparams(("arbitrary",), VMEM_LIMIT),
        name="attn_prep",
    )(proj, proj, proj, proj, gate_logits, gb, positions.reshape(S, 1), inv_full, sgn)


def _compress_kernel(x_ref, w1_ref, w2_ref, pe_ref, o_ref, ot_ref, *, nch):
    x = x_ref[0]
    w1a = w1_ref[0, 0].astype(BF16)
    w1b = w1_ref[0, 1].astype(BF16)
    first = _dot(x, w1a)
    second = _dot(x, w1b)
    pe_a = jnp.broadcast_to(pe_ref[0, 0], (8, x.shape[1])).astype(BF16)
    pe_b = jnp.broadcast_to(pe_ref[0, 1], (8, x.shape[1])).astype(BF16)
    c0 = (_dot(pe_a, w1a) + _dot(pe_b, w1b))[0:1, :]
    hid = _silu(first + pltpu.roll(second, nch - 1, 0) + c0)
    out = _dot(hid.astype(BF16), w2_ref[0].astype(BF16))
    rowi = lax.broadcasted_iota(jnp.int32, out.shape, 0)
    out = jnp.where(rowi < nch - 1, out, 0.0)
    o_ref[0] = out.astype(BF16)
    ot_ref[0] = out.T.astype(BF16)


def nsa_compress(cv_raw, w1, w2, pe):
    n4, S, hd = cv_raw.shape
    G = N_KV_HEADS
    nch = S // CMP_STRIDE
    cw = CMP_STRIDE * hd
    x = cv_raw.reshape(n4, nch, cw)
    w1r = w1.reshape(2, 2, cw, hd)
    per = pe.reshape(2, 2, 1, cw)
    return pl.pallas_call(
        functools.partial(_compress_kernel, nch=nch),
        grid=(n4,),
        in_specs=[pl.BlockSpec((1, nch, cw), lambda i: (i, 0, 0)),
                  pl.BlockSpec((1, 2, cw, hd), lambda i: (i // G, 0, 0, 0)),
                  pl.BlockSpec((1, hd, hd), lambda i: (i // G, 0, 0)),
                  pl.BlockSpec((1, 2, 1, cw), lambda i: (i // G, 0, 0, 0))],
        out_specs=[pl.BlockSpec((1, nch, hd), lambda i: (i, 0, 0)),
                   pl.BlockSpec((1, hd, nch), lambda i: (i, 0, 0))],
        out_shape=[jax.ShapeDtypeStruct((n4, nch, hd), BF16),
                   jax.ShapeDtypeStruct((n4, hd, nch), BF16)],
        compiler_params=_cparams(("arbitrary",), VMEM_LIMIT),
        name="nsa_compress",
    )(x, w1r, w2, per)


CMP_PAD = 8
CMP_CLASS = 256


def _cmp_kernel(q_ref, kc_ref, vct_ref, oc_ref, mask_ref, pad_ref, *, tq, nch, ns, n_sel):
    qb = pl.program_id(1)
    q0 = qb * tq
    t = q0 + lax.broadcasted_iota(jnp.int32, (1, tq), 1)
    ratio = SLC_LEN // CMP_STRIDE

    def run(nr):
        nb = nr // ratio
        n_idx = lax.broadcasted_iota(jnp.int32, (nr, tq), 0)
        visible = (n_idx * CMP_STRIDE + (CMP_LEN - 1)) <= t
        kc = kc_ref[0, 0:nr, :]
        vct = vct_ref[0, :, 0:nr]
        psum = jnp.zeros((nr, tq), F32)
        for r in range(GQA):
            s = jnp.where(visible, _dot_nt(kc, q_ref[r]), MASK_NEG)
            m = jnp.max(s, axis=0, keepdims=True)
            p = jnp.where(visible, jnp.exp(s - m), 0.0)
            denom = jnp.maximum(jnp.sum(p, axis=0, keepdims=True), 1e-30)
            p = p * (1.0 / denom)
            psum = psum + p
            ot = _dot(vct, p.astype(BF16))
            oc_ref[:, r * HEAD_DIM:(r + 1) * HEAD_DIM] = ot.T
        pad_ref[0:CMP_PAD, :] = jnp.zeros((CMP_PAD, tq), F32)
        pad_ref[CMP_PAD:CMP_PAD + nr, :] = psum
        imp = pad_ref[pl.ds(CMP_PAD - 1, nb, stride=ratio), :]
        for r in range(ratio):
            imp = imp + pad_ref[pl.ds(CMP_PAD + r, nb, stride=ratio), :]
        j = lax.broadcasted_iota(jnp.int32, (nb, tq), 0)
        cur = t >> SLC_SHIFT
        allowed = j <= cur
        forced = (j == 0) | (j == cur) | (j == cur - 1)
        score = jnp.where(forced, imp + FORCE_BONUS, imp)
        score = jnp.where(allowed, score, -1.0)
        picked = jnp.zeros((nb, tq), F32)
        for _ in range(min(n_sel, nb)):
            mx = jnp.max(score, axis=0, keepdims=True)
            first = jnp.min(jnp.where(score == mx, j, nb), axis=0, keepdims=True)
            hit = j == first
            picked = jnp.where(hit, 1.0, picked)
            score = jnp.where(hit, -jnp.inf, score)
        dropped = jnp.where(allowed, 1.0 - picked, 1.0)
        mask_ref[0, 0:nb, :] = dropped.astype(BF16)
        if nb < ns:
            mask_ref[0, nb:ns, :] = jnp.ones((ns - nb, tq), BF16)

    need = (q0 + tq) // CMP_STRIDE
    nclass = -(-nch // CMP_CLASS)
    for c in range(nclass):
        @pl.when((need - 1) // CMP_CLASS == c)
        def _():
            run(min((c + 1) * CMP_CLASS, nch))


def nsa_compressed(q_raw, kc, vct, tq=128):
    _, S, hd = q_raw.shape
    G = N_KV_HEADS
    nch = S // CMP_STRIDE
    ns = S // SLC_LEN
    n_sel = min(N_SEL, ns)
    return pl.pallas_call(
        functools.partial(_cmp_kernel, tq=tq, nch=nch, ns=ns, n_sel=n_sel),
        grid=(G, S // tq),
        in_specs=[pl.BlockSpec((GQA, tq, hd), lambda g, i: (g, i, 0)),
                  pl.BlockSpec((1, nch, hd), lambda g, i: (g, 0, 0)),
                  pl.BlockSpec((1, hd, nch), lambda g, i: (G + g, 0, 0))],
        out_specs=[pl.BlockSpec((tq, GQA * hd), lambda g, i: (i, g)),
                   pl.BlockSpec((1, ns, tq), lambda g, i: (g, 0, i))],
        out_shape=[jax.ShapeDtypeStruct((S, D_ATT), F32),
                   jax.ShapeDtypeStruct((G, ns, S), BF16)],
        scratch_shapes=[pltpu.VMEM((nch + CMP_PAD, tq), F32)],
        compiler_params=_cparams(("arbitrary", "arbitrary"), VMEM_LIMIT),
        name="nsa_compressed",
    )(q_raw, kc, vct)


SEL_TK = 512


def _sel_kernel(qt_ref, mask_ref, k_ref, vt_ref, o_ref, qa_ref, s0_ref, s1_ref, p0_ref, p1_ref,
                m_ref, l_ref, acc_ref, *, tq, tk, nparts):
    qb = pl.program_id(1)
    q0 = qb * tq
    cols = GQA * tq
    hd = HEAD_DIM
    last = (q0 + tq - 1) // tk
    tiles_per_part = (LANES * SLC_LEN) // tk
    for part in range(nparts):
        flags = mask_ref[0, part * LANES:(part + 1) * LANES, :]
        for r in range(GQA):
            qa_ref[part, 0:hd, r * tq:(r + 1) * tq] = qt_ref[r]
            qa_ref[part, hd:hd + LANES, r * tq:(r + 1) * tq] = flags
    m_ref[...] = jnp.full((1, cols), MASK_NEG, F32)
    l_ref[...] = jnp.zeros((1, cols), F32)
    acc_ref[...] = jnp.zeros((hd, cols), F32)
    s_bufs = (s0_ref, s1_ref)
    p_bufs = (p0_ref, p1_ref)
    p1_ref[...] = jnp.zeros((tk, cols), BF16)

    def scores(kt):
        k0 = pl.multiple_of(kt * tk, tk)
        return _dot(k_ref[0, pl.ds(k0, tk), :], qa_ref[kt // tiles_per_part])

    def values(kt, p):
        k0 = pl.multiple_of(kt * tk, tk)
        return _dot(vt_ref[0, :, pl.ds(k0, tk)], p)

    def softmax_step(s_buf, p_buf, pv_prev, keep=None):
        sub = 8
        rows = lambda r: slice(r * sub, (r + 1) * sub)
        tile_rows = lambda r: s_buf[rows(r), :] if keep is None else jnp.where(
            keep[rows(r), :], s_buf[rows(r), :], MASK_NEG)
        mx = tile_rows(0)
        for r in range(1, tk // sub):
            mx = jnp.maximum(mx, tile_rows(r))
        m_old = m_ref[...]
        m_new = jnp.maximum(m_old, jnp.max(mx, axis=0, keepdims=True))
        alpha = jnp.exp2(m_old - m_new)
        m_rows = jnp.broadcast_to(m_new, (sub, cols))
        tot = jnp.zeros((sub, cols), F32)
        for r in range(0, tk // sub, 2):
            pa = jnp.exp2(tile_rows(r) - m_rows)
            pb = jnp.exp2(tile_rows(r + 1) - m_rows)
            tot = tot + (pa + pb)
            p_buf[r * sub:(r + 2) * sub, :] = jnp.concatenate([pa, pb], axis=0).astype(BF16)
        l_ref[...] = alpha * l_ref[...] + jnp.sum(tot, axis=0, keepdims=True)
        acc_ref[...] = alpha * (acc_ref[...] + pv_prev)
        m_ref[...] = m_new

    s0_ref[...] = scores(0)

    def step(kt, cur):
        pv_prev = values(jnp.maximum(kt - 1, 0), p_bufs[1 - cur][...])
        s_bufs[1 - cur][...] = scores(kt + 1)
        softmax_step(s_bufs[cur], p_bufs[cur], pv_prev)

    def pair(i, carry):
        step(2 * i, 0)
        step(2 * i + 1, 1)
        return carry

    lax.fori_loop(0, last // 2, pair, 0)

    def finish(cur):
        pv_prev = values(jnp.maximum(last - 1, 0), p_bufs[1 - cur][...])
        key = last * tk + lax.broadcasted_iota(jnp.int32, (tk, 1), 0)
        t_col = q0 + (lax.broadcasted_iota(jnp.int32, (1, cols), 1) & (tq - 1))
        softmax_step(s_bufs[cur], p_bufs[cur], pv_prev, keep=key <= t_col)
        o = (acc_ref[...] + values(last, p_bufs[cur][...])) * (1.0 / l_ref[...])
        for r in range(GQA):
            o_ref[:, r * hd:(r + 1) * hd] = o[:, r * tq:(r + 1) * tq].T

    @pl.when((last & 1) == 1)
    def _():
        step(last - 1, 0)
        finish(1)

    @pl.when((last & 1) == 0)
    def _():
        finish(0)


def nsa_selected(q_rope_t, mask, ks_aug, vs_t, tq=128, tk=SEL_TK):
    _, hd, S = q_rope_t.shape
    G = N_KV_HEADS
    tk = min(tk, S)
    nsp = mask.shape[1]
    nparts = nsp // LANES
    cols = GQA * tq
    return pl.pallas_call(
        functools.partial(_sel_kernel, tq=tq, tk=tk, nparts=nparts),
        grid=(G, S // tq),
        in_specs=[pl.BlockSpec((GQA, hd, tq), lambda g, i: (g, 0, i)),
                  pl.BlockSpec((1, nsp, tq), lambda g, i: (g, 0, i)),
                  pl.BlockSpec((1, S, hd + LANES), lambda g, i: (g, 0, 0)),
                  pl.BlockSpec((1, hd, S), lambda g, i: (g, 0, 0))],
        out_specs=pl.BlockSpec((tq, GQA * hd), lambda g, i: (i, g)),
        out_shape=jax.ShapeDtypeStruct((S, D_ATT), F32),
        scratch_shapes=[pltpu.VMEM((nparts, hd + LANES, cols), BF16),
                        pltpu.VMEM((tk, cols), F32), pltpu.VMEM((tk, cols), F32),
                        pltpu.VMEM((tk, cols), BF16), pltpu.VMEM((tk, cols), BF16),
                        pltpu.VMEM((1, cols), F32),
                        pltpu.VMEM((1, cols), F32),
                        pltpu.VMEM((hd, cols), F32)],
        compiler_params=_cparams(("arbitrary", "arbitrary"), VMEM_LIMIT),
        name="nsa_selected",
    )(q_rope_t, mask, ks_aug, vs_t)


def _win_kernel(qt_ref, k_ref, vt_ref, o_ref, qa_ref, s_ref, p_ref, *, tq, span):
    qb = pl.program_id(1)
    q0 = qb * tq
    cols = GQA * tq
    hd = HEAD_DIM
    sub = 8
    nsl = span // sub
    rows = lambda r: slice(r * sub, (r + 1) * sub)
    start = pl.multiple_of(jnp.maximum(q0 + tq - span, 0), tq)
    for r in range(GQA):
        qa_ref[:, r * tq:(r + 1) * tq] = qt_ref[r]
    s_ref[...] = _dot(k_ref[0, pl.ds(start, span), :], qa_ref[...])
    t_col = q0 + (lax.broadcasted_iota(jnp.int32, (1, cols), 1) & (tq - 1))

    def attend(masked):
        for r in masked:
            diff = t_col - (start + r * sub + lax.broadcasted_iota(jnp.int32, (sub, 1), 0))
            s_ref[rows(r), :] = jnp.where((diff >= 0) & (diff < WINDOW), s_ref[rows(r), :],
                                          MASK_NEG)
        mx = s_ref[rows(0), :]
        for r in range(1, nsl):
            mx = jnp.maximum(mx, s_ref[rows(r), :])
        m_rows = jnp.broadcast_to(jnp.max(mx, axis=0, keepdims=True), (sub, cols))
        tot = jnp.zeros((sub, cols), F32)
        for r in range(0, nsl, 2):
            pa = jnp.exp2(s_ref[rows(r), :] - m_rows)
            pb = jnp.exp2(s_ref[rows(r + 1), :] - m_rows)
            tot = tot + (pa + pb)
            p_ref[r * sub:(r + 2) * sub, :] = jnp.concatenate([pa, pb], axis=0).astype(BF16)
        l = jnp.sum(tot, axis=0, keepdims=True)
        o = _dot(vt_ref[0, :, pl.ds(start, span)], p_ref[...]) * (1.0 / l)
        for r in range(GQA):
            o_ref[:, r * hd:(r + 1) * hd] = o[:, r * tq:(r + 1) * tq].T

    edge = tq // sub
    interior = q0 + tq - span >= 0

    @pl.when(interior)
    def _():
        attend(list(range(edge)) + list(range(nsl - edge, nsl)))

    @pl.when(jnp.logical_not(interior))
    def _():
        attend(list(range(nsl)))


def nsa_window(q_rope_t, kw, vw_t, tq=128):
    _, hd, S = q_rope_t.shape
    G = N_KV_HEADS
    span = min(WINDOW + tq, S)
    cols = GQA * tq
    return pl.pallas_call(
        functools.partial(_win_kernel, tq=tq, span=span),
        grid=(G, S // tq),
        in_specs=[pl.BlockSpec((GQA, hd, tq), lambda g, i: (g, 0, i)),
                  pl.BlockSpec((1, S, hd), lambda g, i: (g, 0, 0)),
                  pl.BlockSpec((1, hd, S), lambda g, i: (g, 0, 0))],
        out_specs=pl.BlockSpec((tq, GQA * hd), lambda g, i: (i, g)),
        out_shape=jax.ShapeDtypeStruct((S, D_ATT), F32),
        scratch_shapes=[pltpu.VMEM((hd, cols), BF16),
                        pltpu.VMEM((span, cols), F32),
                        pltpu.VMEM((span, cols), BF16)],
        compiler_params=_cparams(("arbitrary", "arbitrary"), VMEM_LIMIT),
        name="nsa_window",
    )(q_rope_t, kw, vw_t)


def _gate_kernel(oc_ref, os_ref, ow_ref, gate_ref, g_ref, o_ref, acc_ref):
    hd = HEAD_DIM
    for h in range(N_HEADS):
        sl = slice(h * hd, (h + 1) * hd)
        acc_ref[:, sl] = (gate_ref[:, 3 * h:3 * h + 1] * oc_ref[:, sl]
                          + gate_ref[:, 3 * h + 1:3 * h + 2] * os_ref[:, sl]
                          + gate_ref[:, 3 * h + 2:3 * h + 3] * ow_ref[:, sl])
    o_ref[...] = _rms(acc_ref[...], g_ref[...]).astype(BF16)


def nsa_gate(o_c, o_s, o_w, gates, g, ts=512):
    S = o_c.shape[0]
    blk = pl.BlockSpec((ts, D_ATT), lambda i: (i, 0))
    return pl.pallas_call(
        _gate_kernel,
        grid=(S // ts,),
        in_specs=[blk, blk, blk,
                  pl.BlockSpec((ts, LANES), lambda i: (i, 0)),
                  pl.BlockSpec((1, D_ATT), lambda i: (0, 0))],
        out_specs=blk,
        out_shape=jax.ShapeDtypeStruct((S, D_ATT), BF16),
        scratch_shapes=[pltpu.VMEM((ts, D_ATT), F32)],
        compiler_params=_cparams(("arbitrary",), VMEM_LIMIT),
        name="nsa_gate",
    )(o_c, o_s, o_w, gates, g.reshape(1, D_ATT))


SSM_ROWS = 8
def _ssm_kernel(u_ref, wbr_ref, wbi_ref, wcr_ref, wci_ref, stepr_ref, stepi_ref, powr_ref,
                powi_ref, d_ref, gw_ref, gb_ref, g_ref, o_ref, hr_ref, hi_ref, cr_ref, ci_ref,
                *, ts):
    i = pl.program_id(0)

    @pl.when(i == 0)
    def _():
        cr_ref[...] = jnp.zeros_like(cr_ref)
        ci_ref[...] = jnp.zeros_like(ci_ref)

    u = u_ref[...]
    ub = u.astype(BF16)
    hr_ref[...] = _dot(ub, wbr_ref[...])
    hi_ref[...] = _dot(ub, wbi_ref[...])
    groups = (ts // SSM_ROWS, SSM_ROWS, SSM_N)
    d = 1
    k = 0
    while d < SSM_ROWS:
        hr = hr_ref[...].reshape(groups)
        hi = hi_ref[...].reshape(groups)
        sr = pltpu.roll(hr, d, 1)
        si = pltpu.roll(hi, d, 1)
        ar = stepr_ref[k]
        ai = stepi_ref[k]
        hr_ref[...] = (hr + (ar * sr - ai * si)).reshape(ts, SSM_N)
        hi_ref[...] = (hi + (ar * si + ai * sr)).reshape(ts, SSM_N)
        d *= 2
        k += 1
    cr = cr_ref[...]
    ci = ci_ref[...]
    pr = powr_ref[...]
    pi = powi_ref[...]
    for r in range(ts // SSM_ROWS):
        sl = slice(r * SSM_ROWS, (r + 1) * SSM_ROWS)
        cbr = jnp.broadcast_to(cr, (SSM_ROWS, SSM_N))
        cbi = jnp.broadcast_to(ci, (SSM_ROWS, SSM_N))
        hr = hr_ref[sl, :] + (pr * cbr - pi * cbi)
        hi = hi_ref[sl, :] + (pr * cbi + pi * cbr)
        hr_ref[sl, :] = hr
        hi_ref[sl, :] = hi
        cr = hr[SSM_ROWS - 1:SSM_ROWS, :]
        ci = hi[SSM_ROWS - 1:SSM_ROWS, :]
    cr_ref[...] = cr
    ci_ref[...] = ci
    y = (_dot(hr_ref[...].astype(BF16), wcr_ref[...])
         - _dot(hi_ref[...].astype(BF16), wci_ref[...]))
    y = y + d_ref[...] * u
    gate = _sigmoid(_dot(y.astype(BF16), gw_ref[...].astype(BF16)) + gb_ref[...])
    o_ref[...] = _rms(_gelu_tanh(y) * gate, g_ref[...]).astype(BF16)


def ssm_mixer(proj, lam_re, lam_im, log_dt, b_re, b_im, c_re, c_im, d_skip, glu_w, glu_b, g,
              ts=256):
    S = proj.shape[0]
    NG, P, GC = N_SSM_GROUPS, SSM_STATE, SSM_GROUP
    lr = jnp.minimum(lam_re, -1e-4)
    li = lam_im
    dt = jnp.exp(log_dt)[:, None]
    mag = jnp.exp(lr * dt)
    ar = mag * jnp.cos(li * dt)
    ai = mag * jnp.sin(li * dt)
    den = lr * lr + li * li
    cr = ((ar - 1.0) * lr + ai * li) / den
    ci = (ai * lr - (ar - 1.0) * li) / den
    bbr = cr[..., None] * b_re - ci[..., None] * b_im
    bbi = cr[..., None] * b_im + ci[..., None] * b_re
    eye = jnp.eye(NG, dtype=F32)
    wb = lambda b: jnp.einsum('gpc,gh->gchp', b, eye).reshape(NG * GC, NG * P).astype(BF16)
    wc = lambda c: jnp.einsum('gcp,gh->gphc', c, eye).reshape(NG * P, NG * GC).astype(BF16)

    def powers(n):
        n = n.astype(F32)[:, None]
        lrd = (lr * dt).reshape(1, NG * P)
        lid = (li * dt).reshape(1, NG * P)
        m = jnp.exp(n * lrd)
        return m * jnp.cos(n * lid), m * jnp.sin(n * lid)

    nsteps = int(math.log2(SSM_ROWS))
    dist = 2 ** jnp.arange(nsteps)
    stepr, stepi = powers(dist)
    reach = (jnp.arange(SSM_ROWS)[None, :] >= dist[:, None]).astype(F32)[:, :, None]
    stepr = stepr[:, None, :] * reach
    stepi = stepi[:, None, :] * reach
    powr, powi = powers(jnp.arange(1, SSM_ROWS + 1))
    row = lambda v: v.reshape(1, -1)
    full = lambda a: pl.BlockSpec(a.shape, lambda i: (0,) * a.ndim)
    args = [wb(bbr), wb(bbi), wc(c_re), wc(c_im), stepr, stepi, powr, powi, row(d_skip), glu_w,
            row(glu_b), row(g)]
    return pl.pallas_call(
        functools.partial(_ssm_kernel, ts=ts),
        grid=(S // ts,),
        in_specs=[pl.BlockSpec((ts, D_SSM), lambda i: (i, (D_MAIN - D_SSM) // D_SSM))]
                 + [full(a) for a in args],
        out_specs=pl.BlockSpec((ts, D_SSM), lambda i: (i, 0)),
        out_shape=jax.ShapeDtypeStruct((S, D_SSM), BF16),
        scratch_shapes=[pltpu.VMEM((ts, SSM_N), F32), pltpu.VMEM((ts, SSM_N), F32),
                        pltpu.VMEM((1, SSM_N), F32), pltpu.VMEM((1, SSM_N), F32)],
        compiler_params=_cparams(("arbitrary",), VMEM_LIMIT),
        name="ssm_mixer",
    )(proj, *args)


FFN_TM = 1024
FFN_TF = 256
FFN_SUB = 256


def _swiglu_kernel(te_ref, nv_ref, x_ref, g_ref, wg_ref, wu_ref, wd_ref, o_ref, xn_ref, *, tm):
    i = pl.program_id(0)
    j = pl.program_id(1)
    nvalid = nv_ref[i]
    subs = [slice(sb * FFN_SUB, (sb + 1) * FFN_SUB) for sb in range(tm // FFN_SUB)]

    def weights():
        return wg_ref[0].astype(BF16), wu_ref[0].astype(BF16), wd_ref[0].astype(BF16)

    def normalise(sl):
        xn_ref[sl, :] = _rms(x_ref[sl, :], g_ref[...]).astype(BF16)

    def expert(sl, w):
        x = xn_ref[sl, :]
        a = (_silu(_dot(x, w[0])) * _dot(x, w[1])).astype(BF16)
        return _dot(a, w[2])

    dense = nvalid > tm - FFN_SUB

    @pl.when(dense & (j == 0))
    def _():
        w = weights()
        for sl in subs:
            normalise(sl)
            o_ref[sl, :] = expert(sl, w)

    @pl.when(dense & (j > 0))
    def _():
        w = weights()
        for sl in subs:
            o_ref[sl, :] += expert(sl, w)

    @pl.when(jnp.logical_not(dense))
    def _():
        w = weights()
        for sb, sl in enumerate(subs):
            @pl.when((sb * FFN_SUB < nvalid) & (j == 0))
            def _():
                normalise(sl)
                o_ref[sl, :] = expert(sl, w)

            @pl.when((sb * FFN_SUB < nvalid) & (j > 0))
            def _():
                o_ref[sl, :] += expert(sl, w)

            @pl.when((sb * FFN_SUB >= nvalid) & (j == 0))
            def _():
                o_ref[sl, :] = jnp.zeros((FFN_SUB, o_ref.shape[1]), F32)


def swiglu(x, g, wg, wu, wd, tile_expert, tile_valid, tm=FFN_TM, tf=FFN_TF):
    N, D = x.shape
    F = wg.shape[2]
    tm = min(tm, N)
    last = F // tf - 1

    def wj(i, j, nv):
        return jnp.where(nv[i] > 0, j, last)

    in_specs = [pl.BlockSpec((tm, D), lambda i, j, te, nv: (i, 0)),
                pl.BlockSpec((1, D), lambda i, j, te, nv: (0, 0)),
                pl.BlockSpec((1, D, tf), lambda i, j, te, nv: (te[i], 0, wj(i, j, nv))),
                pl.BlockSpec((1, D, tf), lambda i, j, te, nv: (te[i], 0, wj(i, j, nv))),
                pl.BlockSpec((1, tf, D), lambda i, j, te, nv: (te[i], wj(i, j, nv), 0))]
    return pl.pallas_call(
        functools.partial(_swiglu_kernel, tm=tm),
        grid_spec=pltpu.PrefetchScalarGridSpec(
            num_scalar_prefetch=2,
            grid=(N // tm, F // tf),
            in_specs=in_specs,
            out_specs=pl.BlockSpec((tm, D), lambda i, j, te, nv: (i, 0)),
            scratch_shapes=[pltpu.VMEM((tm, D), BF16)]),
        out_shape=jax.ShapeDtypeStruct((N, D), F32),
        compiler_params=_cparams(("arbitrary", "arbitrary"), VMEM_LIMIT),
        name="swiglu",
    )(tile_expert, tile_valid, x, g.reshape(1, D), wg, wu, wd)


def _route_kernel(lg_ref, info_ref, cnt_ref, run_ref, *, tb):
    i = pl.program_id(0)

    @pl.when(i == 0)
    def _():
        run_ref[...] = jnp.zeros_like(run_ref)

    lane = lax.broadcasted_iota(jnp.int32, (tb, LANES), 1)
    lg = jnp.where(lane < N_EXPERTS, lg_ref[...], -jnp.inf)
    m1 = jnp.max(lg, axis=1, keepdims=True)
    i1 = jnp.min(jnp.where(lg == m1, lane, LANES), axis=1, keepdims=True)
    lg2 = jnp.where(lane == i1, -jnp.inf, lg)
    m2 = jnp.max(lg2, axis=1, keepdims=True)
    i2 = jnp.min(jnp.where(lg2 == m2, lane, LANES), axis=1, keepdims=True)
    e2 = jnp.exp(m2 - m1)
    inv = 1.0 / (1.0 + e2)
    w1 = inv
    w2 = e2 * inv
    onehot = jnp.where((lane == i1) | (lane == i2), 1.0, 0.0)
    r = lax.broadcasted_iota(jnp.int32, (tb, tb), 0)
    c = lax.broadcasted_iota(jnp.int32, (tb, tb), 1)
    tri = jnp.where(c < r, 1.0, 0.0).astype(BF16)
    rank = _dot(tri, onehot.astype(BF16)) + run_ref[...]
    r1 = jnp.sum(jnp.where(lane == i1, rank, 0.0), axis=1, keepdims=True)
    r2 = jnp.sum(jnp.where(lane == i2, rank, 0.0), axis=1, keepdims=True)
    info = jnp.where(lane == 0, i1.astype(F32), 0.0)
    info = jnp.where(lane == 1, i2.astype(F32), info)
    info = jnp.where(lane == 2, r1, info)
    info = jnp.where(lane == 3, r2, info)
    info = jnp.where(lane == 4, w1, info)
    info = jnp.where(lane == 5, w2, info)
    info_ref[...] = info
    run_ref[...] = run_ref[...] + jnp.sum(onehot, axis=0, keepdims=True)
    cnt_ref[...] = run_ref[...]


def moe_route(logits, tb=512):
    S = logits.shape[0]
    return pl.pallas_call(
        functools.partial(_route_kernel, tb=tb),
        grid=(S // tb,),
        in_specs=[pl.BlockSpec((tb, LANES), lambda i: (i, 0))],
        out_specs=[pl.BlockSpec((tb, LANES), lambda i: (i, 0)),
                   pl.BlockSpec((1, LANES), lambda i: (0, 0))],
        out_shape=[jax.ShapeDtypeStruct((S, LANES), F32), jax.ShapeDtypeStruct((1, LANES), F32)],
        scratch_shapes=[pltpu.VMEM((1, LANES), F32)],
        compiler_params=_cparams(("arbitrary",)),
        name="moe_route",
    )(logits)


DISPATCH_TB = 256
SUBLANES = 8
ZERO_ROWS = 128
ZERO_REGIONS = N_EXPERTS + 1


def _dispatch_kernel(pos_ref, fill_ref, x_ref, o_ref, zero_ref, sem, zsem, *, tb, s):
    i = pl.program_id(0)

    def zero_region(first, n, wait):
        def piece(off, rows, aligned):
            if aligned:
                off = pl.multiple_of(off, SUBLANES)
            cp = pltpu.make_async_copy(zero_ref.at[pl.ds(0, rows), :],
                                       o_ref.at[pl.ds(off, rows), :], zsem)
            cp.wait() if wait else cp.start()

        head = jnp.minimum((-first) & (SUBLANES - 1), n)
        lax.fori_loop(0, head, lambda r, c: (piece(first + r, 1, False), c)[1], 0)
        base = first + head
        m = n - head
        big = ZERO_ROWS
        lax.fori_loop(0, m // big, lambda r, c: (piece(base + r * big, big, True), c)[1], 0)
        rem = m % big
        size = big // 2
        while size >= SUBLANES:
            @pl.when((rem & size) != 0)
            def _():
                piece(base + (m - rem) + (rem & ~(2 * size - 1)), size, True)
            size //= 2
        tail0 = base + (m & ~(SUBLANES - 1))
        lax.fori_loop(0, m & (SUBLANES - 1), lambda r, c: (piece(tail0 + r, 1, False), c)[1], 0)

    @pl.when(i == 0)
    def _():
        zero_ref[...] = jnp.zeros_like(zero_ref)
        for wait in (False, True):
            for g in range(ZERO_REGIONS):
                zero_region(fill_ref[g], fill_ref[ZERO_REGIONS + g], wait)

    def copy(r, k):
        dst = pos_ref[k * s + i * tb + r]
        return pltpu.make_async_copy(x_ref.at[pl.ds(r, 1), :], o_ref.at[pl.ds(dst, 1), :], sem)

    def start(r, carry):
        copy(r, 0).start()
        copy(r, 1).start()
        return carry

    lax.fori_loop(0, tb, start, 0)

    def drain(r, carry):
        copy(r, 0).wait()
        copy(r, 1).wait()
        return carry

    lax.fori_loop(0, tb, drain, 0)


def moe_dispatch(x, pos, fill, n_rows, tb=DISPATCH_TB):
    S, D = x.shape
    return pl.pallas_call(
        functools.partial(_dispatch_kernel, tb=tb, s=S),
        grid_spec=pltpu.PrefetchScalarGridSpec(
            num_scalar_prefetch=2,
            grid=(S // tb,),
            in_specs=[pl.BlockSpec((tb, D), lambda i, pos, fill: (i, 0))],
            out_specs=pl.BlockSpec(memory_space=pl.ANY),
            scratch_shapes=[pltpu.VMEM((ZERO_ROWS, D), x.dtype),
                            pltpu.SemaphoreType.DMA(()),
                            pltpu.SemaphoreType.DMA(())]),
        out_shape=jax.ShapeDtypeStruct((n_rows, D), x.dtype),
        compiler_params=_cparams(("arbitrary",)),
        name="moe_dispatch",
    )(pos, fill, x)


COMBINE_TB = 256


def _combine_kernel(pos_ref, h_ref, info_ref, g_ref, y_ref, o_ref, buf_ref, sem, *, tb, s):
    i = pl.program_id(0)

    def copy(r, k):
        src = pos_ref[k * s + i * tb + r]
        return pltpu.make_async_copy(y_ref.at[pl.ds(src, 1), :], buf_ref.at[k, pl.ds(r, 1), :], sem)

    def start(r, carry):
        copy(r, 0).start()
        copy(r, 1).start()
        return carry

    lax.fori_loop(0, tb, start, 0)

    def drain(r, carry):
        copy(r, 0).wait()
        copy(r, 1).wait()
        return carry

    lax.fori_loop(0, tb, drain, 0)
    w1 = info_ref[:, 4:5]
    w2 = info_ref[:, 5:6]
    first_lower = info_ref[:, 0:1] < info_ref[:, 1:2]
    ya = jnp.where(first_lower, w1 * buf_ref[0], w2 * buf_ref[1])
    yb = jnp.where(first_lower, w2 * buf_ref[1], w1 * buf_ref[0])
    o_ref[...] = _rms(h_ref[...] + (ya + yb), g_ref[...])


def moe_combine(h, info, pos, y_sorted, g, tb=COMBINE_TB):
    S, D = h.shape
    return pl.pallas_call(
        functools.partial(_combine_kernel, tb=tb, s=S),
        grid_spec=pltpu.PrefetchScalarGridSpec(
            num_scalar_prefetch=1,
            grid=(S // tb,),
            in_specs=[pl.BlockSpec((tb, D), lambda i, pos: (i, 0)),
                      pl.BlockSpec((tb, LANES), lambda i, pos: (i, 0)),
                      pl.BlockSpec((1, D), lambda i, pos: (0, 0)),
                      pl.BlockSpec(memory_space=pl.ANY)],
            out_specs=pl.BlockSpec((tb, D), lambda i, pos: (i, 0)),
            scratch_shapes=[pltpu.VMEM((2, tb, D), F32), pltpu.SemaphoreType.DMA(())]),
        out_shape=jax.ShapeDtypeStruct((S, D), F32),
        compiler_params=_cparams(("arbitrary",), VMEM_LIMIT),
        name="moe_combine",
    )(pos, h, info, g.reshape(1, D), y_sorted)


def moe_layer(h, norm_g, logits, wg, wu, wd, final_g, tm=FFN_TM):
    S, D = h.shape
    E = N_EXPERTS
    tm = min(tm, S)
    info, counts = moe_route(logits)
    cnt = counts[0, :E].astype(jnp.int32)
    padded = ((cnt + tm - 1) // tm) * tm
    ends = jnp.cumsum(padded)
    offs = ends - padded
    n_tiles = (2 * S) // tm + E
    e12 = info[:, 0:2].astype(jnp.int32)
    r12 = info[:, 2:4].astype(jnp.int32)
    pos = (jnp.take(offs, e12) + r12).T.reshape(2 * S)
    tile_start = jnp.arange(n_tiles, dtype=jnp.int32) * tm
    tile_expert = jnp.minimum(jnp.sum(tile_start[:, None] >= ends[None, :], axis=1), E - 1)
    tile_expert = tile_expert.astype(jnp.int32)
    tile_valid = jnp.clip(jnp.take(offs + cnt, tile_expert) - tile_start, 0, tm).astype(jnp.int32)
    n_rows = n_tiles * tm
    fill = jnp.concatenate([offs + cnt, ends[-1:], padded - cnt, n_rows - ends[-1:]])
    fill = fill.astype(jnp.int32)
    xs = moe_dispatch(h, pos, fill, n_tiles * tm)
    ys = swiglu(xs, norm_g, wg, wu, wd, tile_expert, tile_valid, tm=tm)
    return moe_combine(h, info, pos, ys, final_g)


def _final_norm_kernel(x_ref, g_ref, o_ref):
    o_ref[...] = _rms(x_ref[...], g_ref[...])


def final_norm(x, g, tm=512):
    S, D = x.shape
    return pl.pallas_call(
        _final_norm_kernel,
        grid=(S // tm,),
        in_specs=[pl.BlockSpec((tm, D), lambda i: (i, 0)), pl.BlockSpec((1, D), lambda i: (0, 0))],
        out_specs=pl.BlockSpec((tm, D), lambda i: (i, 0)),
        out_shape=jax.ShapeDtypeStruct((S, D), F32),
        compiler_params=_cparams(("arbitrary",), VMEM_LIMIT),
        name="final_norm",
    )(x, g.reshape(1, D))


def _pad_cols(w, n):
    return jnp.concatenate([w, jnp.zeros((w.shape[0], n - w.shape[1]), w.dtype)], axis=1)


def kernel(x, positions, norm_mix_g, w_in, conv_dw_w, conv_dw_b, conv_ln_g, conv_ln_b, conv_pw_w, conv_pw_b, nsa_gate_b, nsa_pe_k, nsa_pe_v, nsa_w1k, nsa_w2k, nsa_w1v, nsa_w2v, ssm_lambda_re, ssm_lambda_im, ssm_log_dt, ssm_b_re, ssm_b_im, ssm_c_re, ssm_c_im, ssm_d, ssm_glu_w, ssm_glu_b, mix_out_g, w_out, norm_ffn_g, ffn_w_gate, ffn_w_up, ffn_w_down, router_w, moe_w_gate, moe_w_up, moe_w_down, final_norm_g):
    B, S, D = x.shape
    depth = w_in.shape[0]
    outs = []
    for b in range(B):
        h = x[b]
        pos = positions[b]
        pending = None
        out = None
        for l in range(depth):
            w_main = jnp.concatenate([w_in[l][:, :OFF_GATE], w_in[l][:, OFF_SSM:]], axis=1)
            w_gate = _pad_cols(w_in[l][:, OFF_GATE:OFF_SSM], LANES)
            if pending is None:
                hn, gate_logits = norm_side(h, norm_mix_g[l], w_gate, exact_side=False)
            else:
                h, hn, gate_logits = norm_side(h, norm_mix_g[l], w_gate, exact_side=False,
                                               add=pending)
                pending = None
            proj = matmul(hn, w_main)
            g_mix = mix_out_g[l]
            conv_o = conv_module(proj, conv_dw_w[l], conv_dw_b[l], conv_ln_g[l], conv_ln_b[l],
                                 conv_pw_w[l], conv_pw_b[l], g_mix[:D_CONV])
            q_raw, q_rope_t, cv_raw, ks_aug, vs_t, kw, vw_t, gates = attn_prep(
                proj, gate_logits, nsa_gate_b[l], pos)
            kc, kct = nsa_compress(cv_raw, jnp.stack([nsa_w1k[l], nsa_w1v[l]]),
                                   jnp.stack([nsa_w2k[l], nsa_w2v[l]]),
                                   jnp.stack([nsa_pe_k[l], nsa_pe_v[l]]))
            o_c, mask = nsa_compressed(q_raw, kc, kct)
            ns = mask.shape[1]
            if ns % LANES:
                mask = jnp.pad(mask, ((0, 0), (0, LANES - ns % LANES), (0, 0)),
                               constant_values=1.0)
            o_s = nsa_selected(q_rope_t, mask, ks_aug, vs_t)
            o_w = nsa_window(q_rope_t, kw, vw_t)
            att_o = nsa_gate(o_c, o_s, o_w, gates, g_mix[D_CONV:D_CONV + D_ATT])
            ssm_o = ssm_mixer(proj, ssm_lambda_re[l], ssm_lambda_im[l], ssm_log_dt[l],
                              ssm_b_re[l], ssm_b_im[l], ssm_c_re[l], ssm_c_im[l], ssm_d[l],
                              ssm_glu_w[l], ssm_glu_b[l], g_mix[D_CONV + D_ATT:])
            mixed = jnp.concatenate([conv_o, att_o, ssm_o], axis=1)
            h = matmul(mixed, w_out[l], res=h)
            i = l // 2
            if l % 2 == 0:
                n_t = S // min(FFN_TM, S)
                pending = swiglu(h, norm_ffn_g[l], ffn_w_gate[i:i + 1], ffn_w_up[i:i + 1],
                                 ffn_w_down[i:i + 1], jnp.zeros((n_t,), jnp.int32),
                                 jnp.full((n_t,), min(FFN_TM, S), jnp.int32))
            else:
                if l != depth - 1:
                    raise NotImplementedError("a MoE layer is only supported as the last layer")
                (logits,) = norm_side(h, norm_ffn_g[l], _pad_cols(router_w[i], LANES),
                                      exact_side=True, emit_hn=False)
                out = moe_layer(h, norm_ffn_g[l], logits, moe_w_gate[i], moe_w_up[i],
                                moe_w_down[i], final_norm_g)
        if out is None:
            if pending is not None:
                h = h + pending
            out = final_norm(h, final_norm_g)
        outs.append(out)
    return jnp.stack(outs)
```

```python
import functools
import math

import jax
import jax.numpy as jnp
from jax import lax
from jax.experimental import pallas as pl
from jax.experimental.pallas import tpu as pltpu

F32 = jnp.float32
BF16 = jnp.bfloat16

D_MODEL = 2048
D_CONV = 512
D_ATT = 1024
D_SSM = 512
CONV_WIDTH = 31
HEAD_DIM = 128
N_HEADS = 8
N_KV_HEADS = 2
GQA = 4
ROT_DIM = 32
ROPE_THETA = 500000.0
CMP_LEN = 32
CMP_STRIDE = 16
SLC_LEN = 64
SLC_SHIFT = 6
N_SEL = 16
WINDOW = 512
FORCE_BONUS = 1000.0
SSM_GROUP = 16
N_SSM_GROUPS = 32
SSM_STATE = 64
N_EXPERTS = 8
EPS = 1e-5

OFF_Q = 2 * D_CONV
OFF_KV = OFF_Q + D_ATT
OFF_GATE = OFF_KV + 6 * N_KV_HEADS * HEAD_DIM
OFF_SSM = OFF_GATE + 3 * N_HEADS
D_MAIN = 4096

LANES = 128
SUBLANES = 8
MASK_NEG = -1e30
BIAS_NEG = -2e30
LOG2E = math.log2(math.e)
VMEM_LIMIT = 56 * 1024 * 1024

SSM_N = N_SSM_GROUPS * SSM_STATE


def _cparams(sem, vmem=None):
    return pltpu.CompilerParams(dimension_semantics=sem, vmem_limit_bytes=vmem)


def _sigmoid(x):
    return 1.0 / (1.0 + jnp.exp(-x))


def _silu(x):
    return x * _sigmoid(x)


def _gelu_tanh(x):
    return 0.5 * x * (1.0 + jnp.tanh(math.sqrt(2.0 / math.pi) * (x + 0.044715 * (x * x * x))))


def _rms(x, g):
    return x * lax.rsqrt(jnp.mean(x * x, axis=-1, keepdims=True) + EPS) * g


def _dot(a, b):
    return jnp.dot(a, b, preferred_element_type=F32)


def _dot_nt(a, b):
    return lax.dot_general(a, b, (((1,), (1,)), ((), ())), preferred_element_type=F32)


def _norm_side_kernel(*refs, exact_side, has_add, emit_hn):
    refs = list(refs)
    x = refs.pop(0)[...]
    if has_add:
        x = x + refs.pop(0)[...]
    g_ref, w_ref = refs.pop(0), refs.pop(0)
    if has_add:
        refs.pop(0)[...] = x
    y = _rms(x, g_ref[...])
    if emit_hn:
        refs.pop(0)[...] = y.astype(BF16)
    (side_ref,) = refs
    if exact_side:
        side_ref[...] = jnp.dot(y, w_ref[...], preferred_element_type=F32,
                                precision=lax.Precision.HIGHEST)
    else:
        side_ref[...] = _dot(y.astype(BF16), w_ref[...].astype(BF16))


def norm_side(x, g, w_side, exact_side, add=None, emit_hn=True, tm=512):
    S, D = x.shape
    has_add = add is not None
    blk = pl.BlockSpec((tm, D), lambda i: (i, 0))
    in_specs = [blk] + ([blk] if has_add else []) + [pl.BlockSpec((1, D), lambda i: (0, 0)),
                                                     pl.BlockSpec((D, LANES), lambda i: (0, 0))]
    out_specs = (([blk] if has_add else []) + ([blk] if emit_hn else [])
                 + [pl.BlockSpec((tm, LANES), lambda i: (i, 0))])
    out_shape = (([jax.ShapeDtypeStruct((S, D), F32)] if has_add else [])
                 + ([jax.ShapeDtypeStruct((S, D), BF16)] if emit_hn else [])
                 + [jax.ShapeDtypeStruct((S, LANES), F32)])
    args = [x] + ([add] if has_add else []) + [g.reshape(1, D), w_side]
    return pl.pallas_call(
        functools.partial(_norm_side_kernel, exact_side=exact_side, has_add=has_add,
                          emit_hn=emit_hn),
        grid=(S // tm,),
        in_specs=in_specs,
        out_specs=out_specs,
        out_shape=out_shape,
        compiler_params=_cparams(("arbitrary",), VMEM_LIMIT),
        name="norm_side",
    )(*args)


def _mm_kernel(a_ref, w_ref, o_ref):
    o_ref[...] = _dot(a_ref[...], w_ref[...].astype(BF16))


def _mm_res_kernel(a_ref, w_ref, r_ref, o_ref):
    o_ref[...] = r_ref[...] + _dot(a_ref[...], w_ref[...].astype(BF16))


def matmul(a, w, res=None, tm=1024, tn=512):
    M, K = a.shape
    N = w.shape[1]
    tm = min(tm, M)
    in_specs = [pl.BlockSpec((tm, K), lambda j, i: (i, 0)),
                pl.BlockSpec((K, tn), lambda j, i: (0, j))]
    args = [a, w]
    kern = _mm_kernel
    if res is not None:
        in_specs.append(pl.BlockSpec((tm, tn), lambda j, i: (i, j)))
        args.append(res)
        kern = _mm_res_kernel
    return pl.pallas_call(
        kern,
        grid=(N // tn, M // tm),
        in_specs=in_specs,
        out_specs=pl.BlockSpec((tm, tn), lambda j, i: (i, j)),
        out_shape=jax.ShapeDtypeStruct((M, N), F32),
        compiler_params=_cparams(("arbitrary", "arbitrary"), VMEM_LIMIT),
        name="matmul",
    )(*args)


CONV_HALO = 32
CONV_RC = 32


def _conv_kernel(a_ref, b_ref, ah_ref, bh_ref, dww_ref, dwb_ref, lng_ref, lnb_ref, pww_ref,
                 pwb_ref, g_ref, o_ref, u_ref, us_ref, y_ref, *, ts):
    i = pl.program_id(0)
    uh = ah_ref[...] * _sigmoid(bh_ref[...])
    u_ref[0:CONV_HALO, :] = jnp.where(i > 0, uh, 0.0)
    u_ref[CONV_HALO:, :] = a_ref[...] * _sigmoid(b_ref[...])
    lead = CONV_HALO - (CONV_WIDTH - 1)
    span = ts + CONV_HALO - SUBLANES
    for ph in range(1, SUBLANES):
        us_ref[ph - 1] = u_ref[ph:ph + span, :]

    def tap(c, k):
        ph = (lead + k) % SUBLANES
        r0 = c * CONV_RC + (lead + k) - ph
        if ph == 0:
            return u_ref[r0:r0 + CONV_RC, :]
        return us_ref[ph - 1, r0:r0 + CONV_RC, :]

    for c in range(ts // CONV_RC):
        acc = dww_ref[0:1, :] * tap(c, 0)
        for k in range(1, CONV_WIDTH):
            acc = acc + dww_ref[k:k + 1, :] * tap(c, k)
        y_ref[c * CONV_RC:(c + 1) * CONV_RC, :] = acc + dwb_ref[...]
    y = y_ref[...]
    mu = jnp.mean(y, axis=-1, keepdims=True)
    yc = y - mu
    var = jnp.mean(yc * yc, axis=-1, keepdims=True)
    z = _silu(yc * lax.rsqrt(var + EPS) * lng_ref[...] + lnb_ref[...])
    o = _dot(z.astype(BF16), pww_ref[...].astype(BF16)) + pwb_ref[...]
    o_ref[...] = _rms(o, g_ref[...]).astype(BF16)


def conv_module(proj, dw_w, dw_b, ln_g, ln_b, pw_w, pw_b, g, ts=512):
    S = proj.shape[0]
    C = D_CONV
    hb = ts // CONV_HALO
    row = lambda v: v.reshape(1, C)
    vec = pl.BlockSpec((1, C), lambda i: (0, 0))
    halo = lambda col: pl.BlockSpec((CONV_HALO, C), lambda i: (jnp.maximum(i * hb - 1, 0), col))
    return pl.pallas_call(
        functools.partial(_conv_kernel, ts=ts),
        grid=(S // ts,),
        in_specs=[pl.BlockSpec((ts, C), lambda i: (i, 0)),
                  pl.BlockSpec((ts, C), lambda i: (i, 1)),
                  halo(0), halo(1),
                  pl.BlockSpec((CONV_WIDTH, C), lambda i: (0, 0)),
                  vec, vec, vec,
                  pl.BlockSpec((C, C), lambda i: (0, 0)),
                  vec, vec],
        out_specs=pl.BlockSpec((ts, C), lambda i: (i, 0)),
        out_shape=jax.ShapeDtypeStruct((S, C), BF16),
        scratch_shapes=[pltpu.VMEM((ts + CONV_HALO, C), F32),
                        pltpu.VMEM((SUBLANES - 1, ts + CONV_HALO - SUBLANES, C), F32),
                        pltpu.VMEM((ts, C), F32)],
        compiler_params=_cparams(("arbitrary",), VMEM_LIMIT),
        name="conv_module",
    )(proj, proj, proj, proj, dw_w, row(dw_b), row(ln_g), row(ln_b), pw_w, row(pw_b), row(g))


def _prep_kernel(q_ref, kva_ref, kvb_ref, kvc_ref, gl_ref, gb_ref, pos_ref, inv_ref, sgn_ref,
                 qraw_ref, qropet_ref, cv_ref, ks_ref, vst_ref, kw_ref, vwt_ref,
                 gate_ref, *, ts):
    i = pl.program_id(0)
    ang = pos_ref[...].astype(F32) * inv_ref[...]
    cos = jnp.cos(ang)
    sin = jnp.sin(ang) * sgn_ref[...]
    lane = lax.broadcasted_iota(jnp.int32, (ts, LANES), 1)
    half = ROT_DIM // 2

    def rope(x):
        sw = jnp.where(lane < half, pltpu.roll(x, LANES - half, 1), pltpu.roll(x, half, 1))
        return x * cos + sw * sin

    scale = HEAD_DIM ** -0.5
    for h in range(N_HEADS):
        q = q_ref[:, h * HEAD_DIM:(h + 1) * HEAD_DIM]
        qraw_ref[h] = (q * scale).astype(BF16)
        qropet_ref[h] = (rope(q) * (scale * LOG2E)).T.astype(BF16)
    key = i * ts + lax.broadcasted_iota(jnp.int32, (ts, LANES), 0)
    bias = jnp.where(lane == ((key >> SLC_SHIFT) & (LANES - 1)), BIAS_NEG, 0.0).astype(BF16)
    for g in range(N_KV_HEADS):
        lo = g * HEAD_DIM
        hi = (N_KV_HEADS + g) * HEAD_DIM
        cv_ref[g] = kva_ref[:, lo:lo + HEAD_DIM].astype(BF16)
        cv_ref[N_KV_HEADS + g] = kva_ref[:, hi:hi + HEAD_DIM].astype(BF16)
        ks_ref[g, :, 0:HEAD_DIM] = rope(kvb_ref[:, lo:lo + HEAD_DIM]).astype(BF16)
        ks_ref[g, :, HEAD_DIM:2 * HEAD_DIM] = bias
        vst_ref[g] = kvb_ref[:, hi:hi + HEAD_DIM].T.astype(BF16)
        kw_ref[g] = rope(kvc_ref[:, lo:lo + HEAD_DIM]).astype(BF16)
        vwt_ref[g] = kvc_ref[:, hi:hi + HEAD_DIM].T.astype(BF16)
    gate_ref[...] = _sigmoid(gl_ref[...] + gb_ref[...])


def attn_prep(proj, gate_logits, gate_b, positions, ts=256):
    S = proj.shape[0]
    G = N_KV_HEADS
    inv = ROPE_THETA ** (-jnp.arange(0, ROT_DIM, 2, dtype=F32) / ROT_DIM)
    inv_full = jnp.concatenate([inv, inv, jnp.zeros((LANES - ROT_DIM,), F32)]).reshape(1, LANES)
    half = ROT_DIM // 2
    sgn = jnp.concatenate([-jnp.ones((half,), F32), jnp.ones((half,), F32),
                           jnp.zeros((LANES - ROT_DIM,), F32)]).reshape(1, LANES)
    gb = jnp.concatenate([gate_b, jnp.zeros((LANES - gate_b.shape[0],), F32)]).reshape(1, LANES)
    kvw = 2 * G * HEAD_DIM
    kv0 = OFF_KV // kvw
    row = pl.BlockSpec((1, LANES), lambda i: (0, 0))
    hm = lambda n, w: pl.BlockSpec((n, ts, w), lambda i: (0, i, 0))
    hmt = lambda n: pl.BlockSpec((n, HEAD_DIM, ts), lambda i: (0, 0, i))
    return pl.pallas_call(
        functools.partial(_prep_kernel, ts=ts),
        grid=(S // ts,),
        in_specs=[pl.BlockSpec((ts, D_ATT), lambda i: (i, OFF_Q // D_ATT)),
                  pl.BlockSpec((ts, kvw), lambda i: (i, kv0)),
                  pl.BlockSpec((ts, kvw), lambda i: (i, kv0 + 1)),
                  pl.BlockSpec((ts, kvw), lambda i: (i, kv0 + 2)),
                  pl.BlockSpec((ts, LANES), lambda i: (i, 0)),
                  row,
                  pl.BlockSpec((ts, 1), lambda i: (i, 0)),
                  row, row],
        out_specs=[hm(N_HEADS, HEAD_DIM), hmt(N_HEADS), hm(2 * G, HEAD_DIM),
                   hm(G, 2 * HEAD_DIM), hmt(G), hm(G, HEAD_DIM), hmt(G),
                   pl.BlockSpec((ts, LANES), lambda i: (i, 0))],
        out_shape=[jax.ShapeDtypeStruct((N_HEADS, S, HEAD_DIM), BF16),
                   jax.ShapeDtypeStruct((N_HEADS, HEAD_DIM, S), BF16),
                   jax.ShapeDtypeStruct((2 * G, S, HEAD_DIM), BF16),
                   jax.ShapeDtypeStruct((G, S, 2 * HEAD_DIM), BF16),
                   jax.ShapeDtypeStruct((G, HEAD_DIM, S), BF16),
                   jax.ShapeDtypeStruct((G, S, HEAD_DIM), BF16),
                   jax.ShapeDtypeStruct((G, HEAD_DIM, S), BF16),
                   jax.ShapeDtypeStruct((S, LANES), F32)],
        compiler_params=_cparams(("arbitrary",), VMEM_LIMIT),
        name="attn_prep",
    )(proj, proj, proj, proj, gate_logits, gb, positions.reshape(S, 1), inv_full, sgn)


def _compress_kernel(x_ref, w1_ref, w2_ref, pe_ref, o_ref, ot_ref, *, nch):
    x = x_ref[0]
    w1a = w1_ref[0, 0].astype(BF16)
    w1b = w1_ref[0, 1].astype(BF16)
    first = _dot(x, w1a)
    second = _dot(x, w1b)
    pe_a = jnp.broadcast_to(pe_ref[0, 0], (8, x.shape[1])).astype(BF16)
    pe_b = jnp.broadcast_to(pe_ref[0, 1], (8, x.shape[1])).astype(BF16)
    c0 = (_dot(pe_a, w1a) + _dot(pe_b, w1b))[0:1, :]
    hid = _silu(first + pltpu.roll(second, nch - 1, 0) + c0)
    out = _dot(hid.astype(BF16), w2_ref[0].astype(BF16))
    rowi = lax.broadcasted_iota(jnp.int32, out.shape, 0)
    out = jnp.where(rowi < nch - 1, out, 0.0)
    o_ref[0] = out.astype(BF16)
    ot_ref[0] = out.T.astype(BF16)


def nsa_compress(cv_raw, w1, w2, pe):
    n4, S, hd = cv_raw.shape
    G = N_KV_HEADS
    nch = S // CMP_STRIDE
    cw = CMP_STRIDE * hd
    x = cv_raw.reshape(n4, nch, cw)
    w1r = w1.reshape(2, 2, cw, hd)
    per = pe.reshape(2, 2, 1, cw)
    return pl.pallas_call(
        functools.partial(_compress_kernel, nch=nch),
        grid=(n4,),
        in_specs=[pl.BlockSpec((1, nch, cw), lambda i: (i, 0, 0)),
                  pl.BlockSpec((1, 2, cw, hd), lambda i: (i // G, 0, 0, 0)),
                  pl.BlockSpec((1, hd, hd), lambda i: (i // G, 0, 0)),
                  pl.BlockSpec((1, 2, 1, cw), lambda i: (i // G, 0, 0, 0))],
        out_specs=[pl.BlockSpec((1, nch, hd), lambda i: (i, 0, 0)),
                   pl.BlockSpec((1, hd, nch), lambda i: (i, 0, 0))],
        out_shape=[jax.ShapeDtypeStruct((n4, nch, hd), BF16),
                   jax.ShapeDtypeStruct((n4, hd, nch), BF16)],
        compiler_params=_cparams(("arbitrary",), VMEM_LIMIT),
        name="nsa_compress",
    )(x, w1r, w2, per)


CMP_PAD = 8
CMP_CLASS = 256


def _cmp_kernel(q_ref, kc_ref, vct_ref, oc_ref, mask_ref, pad_ref, *, tq, nch, ns, n_sel):
    qb = pl.program_id(1)
    q0 = qb * tq
    t = q0 + lax.broadcasted_iota(jnp.int32, (1, tq), 1)
    ratio = SLC_LEN // CMP_STRIDE

    def run(nr):
        nb = nr // ratio
        n_idx = lax.broadcasted_iota(jnp.int32, (nr, tq), 0)
        visible = (n_idx * CMP_STRIDE + (CMP_LEN - 1)) <= t
        kc = kc_ref[0, 0:nr, :]
        vct = vct_ref[0, :, 0:nr]
        psum = jnp.zeros((nr, tq), F32)
        for r in range(GQA):
            s = jnp.where(visible, _dot_nt(kc, q_ref[r]), MASK_NEG)
            m = jnp.max(s, axis=0, keepdims=True)
            p = jnp.where(visible, jnp.exp(s - m), 0.0)
            denom = jnp.maximum(jnp.sum(p, axis=0, keepdims=True), 1e-30)
            p = p * (1.0 / denom)
            psum = psum + p
            ot = _dot(vct, p.astype(BF16))
            oc_ref[:, r * HEAD_DIM:(r + 1) * HEAD_DIM] = ot.T
        pad_ref[0:CMP_PAD, :] = jnp.zeros((CMP_PAD, tq), F32)
        pad_ref[CMP_PAD:CMP_PAD + nr, :] = psum
        imp = pad_ref[pl.ds(CMP_PAD - 1, nb, stride=ratio), :]
        for r in range(ratio):
            imp = imp + pad_ref[pl.ds(CMP_PAD + r, nb, stride=ratio), :]
        j = lax.broadcasted_iota(jnp.int32, (nb, tq), 0)
        cur = t >> SLC_SHIFT
        allowed = j <= cur
        forced = (j == 0) | (j == cur) | (j == cur - 1)
        score = jnp.where(forced, imp + FORCE_BONUS, imp)
        score = jnp.where(allowed, score, -1.0)
        picked = jnp.zeros((nb, tq), F32)
        for _ in range(min(n_sel, nb)):
            mx = jnp.max(score, axis=0, keepdims=True)
            first = jnp.min(jnp.where(score == mx, j, nb), axis=0, keepdims=True)
            hit = j == first
            picked = jnp.where(hit, 1.0, picked)
            score = jnp.where(hit, -jnp.inf, score)
        dropped = jnp.where(allowed, 1.0 - picked, 1.0)
        mask_ref[0, 0:nb, :] = dropped.astype(BF16)
        if nb < ns:
            mask_ref[0, nb:ns, :] = jnp.ones((ns - nb, tq), BF16)

    need = (q0 + tq) // CMP_STRIDE
    nclass = -(-nch // CMP_CLASS)
    for c in range(nclass):
        @pl.when((need - 1) // CMP_CLASS == c)
        def _():
            run(min((c + 1) * CMP_CLASS, nch))


def nsa_compressed(q_raw, kc, vct, tq=128):
    _, S, hd = q_raw.shape
    G = N_KV_HEADS
    nch = S // CMP_STRIDE
    ns = S // SLC_LEN
    n_sel = min(N_SEL, ns)
    return pl.pallas_call(
        functools.partial(_cmp_kernel, tq=tq, nch=nch, ns=ns, n_sel=n_sel),
        grid=(G, S // tq),
        in_specs=[pl.BlockSpec((GQA, tq, hd), lambda g, i: (g, i, 0)),
                  pl.BlockSpec((1, nch, hd), lambda g, i: (g, 0, 0)),
                  pl.BlockSpec((1, hd, nch), lambda g, i: (G + g, 0, 0))],
        out_specs=[pl.BlockSpec((tq, GQA * hd), lambda g, i: (i, g)),
                   pl.BlockSpec((1, ns, tq), lambda g, i: (g, 0, i))],
        out_shape=[jax.ShapeDtypeStruct((S, D_ATT), F32),
                   jax.ShapeDtypeStruct((G, ns, S), BF16)],
        scratch_shapes=[pltpu.VMEM((nch + CMP_PAD, tq), F32)],
        compiler_params=_cparams(("arbitrary", "arbitrary"), VMEM_LIMIT),
        name="nsa_compressed",
    )(q_raw, kc, vct)


SEL_TK = 512
SEL_TQ = 128


def _sel_kernel(qt_ref, mask_ref, k_ref, vt_ref, o_ref, qa_ref, s0_ref, s1_ref, p0_ref, p1_ref,
                m_ref, l_ref, acc_ref, *, tq, tk, nparts):
    qb = pl.program_id(1)
    q0 = qb * tq
    cols = GQA * tq
    hd = HEAD_DIM
    last = (q0 + tq - 1) // tk
    tiles_per_part = (LANES * SLC_LEN) // tk
    for part in range(nparts):
        flags = mask_ref[0, part * LANES:(part + 1) * LANES, :]
        for r in range(GQA):
            qa_ref[part, 0:hd, r * tq:(r + 1) * tq] = qt_ref[r]
            qa_ref[part, hd:hd + LANES, r * tq:(r + 1) * tq] = flags
    m_ref[...] = jnp.full((1, cols), MASK_NEG, F32)
    l_ref[...] = jnp.zeros((1, cols), F32)
    acc_ref[...] = jnp.zeros((hd, cols), F32)
    s_bufs = (s0_ref, s1_ref)
    p_bufs = (p0_ref, p1_ref)
    p1_ref[...] = jnp.zeros((tk, cols), BF16)

    def scores(kt):
        k0 = pl.multiple_of(kt * tk, tk)
        return _dot(k_ref[0, pl.ds(k0, tk), :], qa_ref[kt // tiles_per_part])

    def values(kt, p):
        k0 = pl.multiple_of(kt * tk, tk)
        return _dot(vt_ref[0, :, pl.ds(k0, tk)], p)

    def softmax_step(s_buf, p_buf, pv_prev, keep=None):
        sub = 8
        rows = lambda r: slice(r * sub, (r + 1) * sub)
        tile_rows = lambda r: s_buf[rows(r), :] if keep is None else jnp.where(
            keep[rows(r), :], s_buf[rows(r), :], MASK_NEG)
        mx = tile_rows(0)
        for r in range(1, tk // sub):
            mx = jnp.maximum(mx, tile_rows(r))
        m_old = m_ref[...]
        m_new = jnp.maximum(m_old, jnp.max(mx, axis=0, keepdims=True))
        alpha = jnp.exp2(m_old - m_new)
        m_rows = jnp.broadcast_to(m_new, (sub, cols))
        tot = jnp.zeros((sub, cols), F32)
        for r in range(0, tk // sub, 2):
            pa = jnp.exp2(tile_rows(r) - m_rows)
            pb = jnp.exp2(tile_rows(r + 1) - m_rows)
            tot = tot + (pa + pb)
            p_buf[r * sub:(r + 2) * sub, :] = jnp.concatenate([pa, pb], axis=0).astype(BF16)
        l_ref[...] = alpha * l_ref[...] + jnp.sum(tot, axis=0, keepdims=True)
        acc_ref[...] = alpha * (acc_ref[...] + pv_prev)
        m_ref[...] = m_new

    s0_ref[...] = scores(0)

    def step(kt, cur):
        pv_prev = values(jnp.maximum(kt - 1, 0), p_bufs[1 - cur][...])
        s_bufs[1 - cur][...] = scores(kt + 1)
        softmax_step(s_bufs[cur], p_bufs[cur], pv_prev)

    def pair(i, carry):
        step(2 * i, 0)
        step(2 * i + 1, 1)
        return carry

    lax.fori_loop(0, last // 2, pair, 0)

    def finish(cur):
        pv_prev = values(jnp.maximum(last - 1, 0), p_bufs[1 - cur][...])
        key = last * tk + lax.broadcasted_iota(jnp.int32, (tk, 1), 0)
        t_col = q0 + (lax.broadcasted_iota(jnp.int32, (1, cols), 1) & (tq - 1))
        softmax_step(s_bufs[cur], p_bufs[cur], pv_prev, keep=key <= t_col)
        o = (acc_ref[...] + values(last, p_bufs[cur][...])) * (1.0 / l_ref[...])
        for r in range(GQA):
            o_ref[:, r * hd:(r + 1) * hd] = o[:, r * tq:(r + 1) * tq].T

    @pl.when((last & 1) == 1)
    def _():
        step(last - 1, 0)
        finish(1)

    @pl.when((last & 1) == 0)
    def _():
        finish(0)


def nsa_selected(q_rope_t, mask, ks_aug, vs_t, tq=SEL_TQ, tk=SEL_TK):
    _, hd, S = q_rope_t.shape
    G = N_KV_HEADS
    tk = min(tk, S)
    nsp = mask.shape[1]
    nparts = nsp // LANES
    cols = GQA * tq
    return pl.pallas_call(
        functools.partial(_sel_kernel, tq=tq, tk=tk, nparts=nparts),
        grid=(G, S // tq),
        in_specs=[pl.BlockSpec((GQA, hd, tq), lambda g, i: (g, 0, i)),
                  pl.BlockSpec((1, nsp, tq), lambda g, i: (g, 0, i)),
                  pl.BlockSpec((1, S, hd + LANES), lambda g, i: (g, 0, 0)),
                  pl.BlockSpec((1, hd, S), lambda g, i: (g, 0, 0))],
        out_specs=pl.BlockSpec((tq, GQA * hd), lambda g, i: (i, g)),
        out_shape=jax.ShapeDtypeStruct((S, D_ATT), F32),
        scratch_shapes=[pltpu.VMEM((nparts, hd + LANES, cols), BF16),
                        pltpu.VMEM((tk, cols), F32), pltpu.VMEM((tk, cols), F32),
                        pltpu.VMEM((tk, cols), BF16), pltpu.VMEM((tk, cols), BF16),
                        pltpu.VMEM((1, cols), F32),
                        pltpu.VMEM((1, cols), F32),
                        pltpu.VMEM((hd, cols), F32)],
        compiler_params=_cparams(("arbitrary", "arbitrary"), VMEM_LIMIT),
        name="nsa_selected",
    )(q_rope_t, mask, ks_aug, vs_t)


def _win_kernel(qt_ref, k_ref, vt_ref, o_ref, qa_ref, s_ref, p_ref, *, tq, span):
    qb = pl.program_id(1)
    q0 = qb * tq
    cols = GQA * tq
    hd = HEAD_DIM
    sub = 8
    nsl = span // sub
    rows = lambda r: slice(r * sub, (r + 1) * sub)
    start = pl.multiple_of(jnp.maximum(q0 + tq - span, 0), tq)
    for r in range(GQA):
        qa_ref[:, r * tq:(r + 1) * tq] = qt_ref[r]
    s_ref[...] = _dot(k_ref[0, pl.ds(start, span), :], qa_ref[...])
    t_col = q0 + (lax.broadcasted_iota(jnp.int32, (1, cols), 1) & (tq - 1))

    def attend(masked):
        for r in masked:
            diff = t_col - (start + r * sub + lax.broadcasted_iota(jnp.int32, (sub, 1), 0))
            s_ref[rows(r), :] = jnp.where((diff >= 0) & (diff < WINDOW), s_ref[rows(r), :],
                                          MASK_NEG)
        mx = s_ref[rows(0), :]
        for r in range(1, nsl):
            mx = jnp.maximum(mx, s_ref[rows(r), :])
        m_rows = jnp.broadcast_to(jnp.max(mx, axis=0, keepdims=True), (sub, cols))
        tot = jnp.zeros((sub, cols), F32)
        for r in range(0, nsl, 2):
            pa = jnp.exp2(s_ref[rows(r), :] - m_rows)
            pb = jnp.exp2(s_ref[rows(r + 1), :] - m_rows)
            tot = tot + (pa + pb)
            p_ref[r * sub:(r + 2) * sub, :] = jnp.concatenate([pa, pb], axis=0).astype(BF16)
        l = jnp.sum(tot, axis=0, keepdims=True)
        o = _dot(vt_ref[0, :, pl.ds(start, span)], p_ref[...]) * (1.0 / l)
        for r in range(GQA):
            o_ref[:, r * hd:(r + 1) * hd] = o[:, r * tq:(r + 1) * tq].T

    edge = tq // sub
    interior = q0 + tq - span >= 0

    @pl.when(interior)
    def _():
        attend(list(range(edge)) + list(range(nsl - edge, nsl)))

    @pl.when(jnp.logical_not(interior))
    def _():
        attend(list(range(nsl)))


def nsa_window(q_rope_t, kw, vw_t, tq=128):
    _, hd, S = q_rope_t.shape
    G = N_KV_HEADS
    span = min(WINDOW + tq, S)
    cols = GQA * tq
    return pl.pallas_call(
        functools.partial(_win_kernel, tq=tq, span=span),
        grid=(G, S // tq),
        in_specs=[pl.BlockSpec((GQA, hd, tq), lambda g, i: (g, 0, i)),
                  pl.BlockSpec((1, S, hd), lambda g, i: (g, 0, 0)),
                  pl.BlockSpec((1, hd, S), lambda g, i: (g, 0, 0))],
        out_specs=pl.BlockSpec((tq, GQA * hd), lambda g, i: (i, g)),
        out_shape=jax.ShapeDtypeStruct((S, D_ATT), F32),
        scratch_shapes=[pltpu.VMEM((hd, cols), BF16),
                        pltpu.VMEM((span, cols), F32),
                        pltpu.VMEM((span, cols), BF16)],
        compiler_params=_cparams(("arbitrary", "arbitrary"), VMEM_LIMIT),
        name="nsa_window",
    )(q_rope_t, kw, vw_t)


def _gate_kernel(oc_ref, os_ref, ow_ref, gate_ref, g_ref, o_ref, acc_ref):
    hd = HEAD_DIM
    for h in range(N_HEADS):
        sl = slice(h * hd, (h + 1) * hd)
        acc_ref[:, sl] = (gate_ref[:, 3 * h:3 * h + 1] * oc_ref[:, sl]
                          + gate_ref[:, 3 * h + 1:3 * h + 2] * os_ref[:, sl]
                          + gate_ref[:, 3 * h + 2:3 * h + 3] * ow_ref[:, sl])
    o_ref[...] = _rms(acc_ref[...], g_ref[...]).astype(BF16)


def nsa_gate(o_c, o_s, o_w, gates, g, ts=512):
    S = o_c.shape[0]
    blk = pl.BlockSpec((ts, D_ATT), lambda i: (i, 0))
    return pl.pallas_call(
        _gate_kernel,
        grid=(S // ts,),
        in_specs=[blk, blk, blk,
                  pl.BlockSpec((ts, LANES), lambda i: (i, 0)),
                  pl.BlockSpec((1, D_ATT), lambda i: (0, 0))],
        out_specs=blk,
        out_shape=jax.ShapeDtypeStruct((S, D_ATT), BF16),
        scratch_shapes=[pltpu.VMEM((ts, D_ATT), F32)],
        compiler_params=_cparams(("arbitrary",), VMEM_LIMIT),
        name="nsa_gate",
    )(o_c, o_s, o_w, gates, g.reshape(1, D_ATT))


SSM_ROWS = 8
def _ssm_kernel(u_ref, wbr_ref, wbi_ref, wcr_ref, wci_ref, stepr_ref, stepi_ref, powr_ref,
                powi_ref, d_ref, gw_ref, gb_ref, g_ref, o_ref, hr_ref, hi_ref, cr_ref, ci_ref,
                *, ts):
    i = pl.program_id(0)

    @pl.when(i == 0)
    def _():
        cr_ref[...] = jnp.zeros_like(cr_ref)
        ci_ref[...] = jnp.zeros_like(ci_ref)

    u = u_ref[...]
    ub = u.astype(BF16)
    hr_ref[...] = _dot(ub, wbr_ref[...])
    hi_ref[...] = _dot(ub, wbi_ref[...])
    groups = (ts // SSM_ROWS, SSM_ROWS, SSM_N)
    d = 1
    k = 0
    while d < SSM_ROWS:
        hr = hr_ref[...].reshape(groups)
        hi = hi_ref[...].reshape(groups)
        sr = pltpu.roll(hr, d, 1)
        si = pltpu.roll(hi, d, 1)
        ar = stepr_ref[k]
        ai = stepi_ref[k]
        hr_ref[...] = (hr + (ar * sr - ai * si)).reshape(ts, SSM_N)
        hi_ref[...] = (hi + (ar * si + ai * sr)).reshape(ts, SSM_N)
        d *= 2
        k += 1
    cr = cr_ref[...]
    ci = ci_ref[...]
    pr = powr_ref[...]
    pi = powi_ref[...]
    for r in range(ts // SSM_ROWS):
        sl = slice(r * SSM_ROWS, (r + 1) * SSM_ROWS)
        cbr = jnp.broadcast_to(cr, (SSM_ROWS, SSM_N))
        cbi = jnp.broadcast_to(ci, (SSM_ROWS, SSM_N))
        hr = hr_ref[sl, :] + (pr * cbr - pi * cbi)
        hi = hi_ref[sl, :] + (pr * cbi + pi * cbr)
        hr_ref[sl, :] = hr
        hi_ref[sl, :] = hi
        cr = hr[SSM_ROWS - 1:SSM_ROWS, :]
        ci = hi[SSM_ROWS - 1:SSM_ROWS, :]
    cr_ref[...] = cr
    ci_ref[...] = ci
    y = (_dot(hr_ref[...].astype(BF16), wcr_ref[...])
         - _dot(hi_ref[...].astype(BF16), wci_ref[...]))
    y = y + d_ref[...] * u
    gate = _sigmoid(_dot(y.astype(BF16), gw_ref[...].astype(BF16)) + gb_ref[...])
    o_ref[...] = _rms(_gelu_tanh(y) * gate, g_ref[...]).astype(BF16)


def ssm_mixer(proj, lam_re, lam_im, log_dt, b_re, b_im, c_re, c_im, d_skip, glu_w, glu_b, g,
              ts=256):
    S = proj.shape[0]
    NG, P, GC = N_SSM_GROUPS, SSM_STATE, SSM_GROUP
    lr = jnp.minimum(lam_re, -1e-4)
    li = lam_im
    dt = jnp.exp(log_dt)[:, None]
    mag = jnp.exp(lr * dt)
    ar = mag * jnp.cos(li * dt)
    ai = mag * jnp.sin(li * dt)
    den = lr * lr + li * li
    cr = ((ar - 1.0) * lr + ai * li) / den
    ci = (ai * lr - (ar - 1.0) * li) / den
    bbr = cr[..., None] * b_re - ci[..., None] * b_im
    bbi = cr[..., None] * b_im + ci[..., None] * b_re
    eye = jnp.eye(NG, dtype=F32)
    wb = lambda b: jnp.einsum('gpc,gh->gchp', b, eye).reshape(NG * GC, NG * P).astype(BF16)
    wc = lambda c: jnp.einsum('gcp,gh->gphc', c, eye).reshape(NG * P, NG * GC).astype(BF16)

    def powers(n):
        n = n.astype(F32)[:, None]
        lrd = (lr * dt).reshape(1, NG * P)
        lid = (li * dt).reshape(1, NG * P)
        m = jnp.exp(n * lrd)
        return m * jnp.cos(n * lid), m * jnp.sin(n * lid)

    nsteps = int(math.log2(SSM_ROWS))
    dist = 2 ** jnp.arange(nsteps)
    stepr, stepi = powers(dist)
    reach = (jnp.arange(SSM_ROWS)[None, :] >= dist[:, None]).astype(F32)[:, :, None]
    stepr = stepr[:, None, :] * reach
    stepi = stepi[:, None, :] * reach
    powr, powi = powers(jnp.arange(1, SSM_ROWS + 1))
    row = lambda v: v.reshape(1, -1)
    full = lambda a: pl.BlockSpec(a.shape, lambda i: (0,) * a.ndim)
    args = [wb(bbr), wb(bbi), wc(c_re), wc(c_im), stepr, stepi, powr, powi, row(d_skip), glu_w,
            row(glu_b), row(g)]
    return pl.pallas_call(
        functools.partial(_ssm_kernel, ts=ts),
        grid=(S // ts,),
        in_specs=[pl.BlockSpec((ts, D_SSM), lambda i: (i, (D_MAIN - D_SSM) // D_SSM))]
                 + [full(a) for a in args],
        out_specs=pl.BlockSpec((ts, D_SSM), lambda i: (i, 0)),
        out_shape=jax.ShapeDtypeStruct((S, D_SSM), BF16),
        scratch_shapes=[pltpu.VMEM((ts, SSM_N), F32), pltpu.VMEM((ts, SSM_N), F32),
                        pltpu.VMEM((1, SSM_N), F32), pltpu.VMEM((1, SSM_N), F32)],
        compiler_params=_cparams(("arbitrary",), VMEM_LIMIT),
        name="ssm_mixer",
    )(proj, *args)


FFN_TM = 1024
FFN_TF = 256
FFN_SUB = 512


def _swiglu_kernel(te_ref, nv_ref, x_ref, g_ref, wg_ref, wu_ref, wd_ref, o_ref, xn_ref, *, tm):
    i = pl.program_id(0)
    j = pl.program_id(1)
    nvalid = nv_ref[i]
    subs = [slice(sb * FFN_SUB, (sb + 1) * FFN_SUB) for sb in range(tm // FFN_SUB)]

    def weights():
        return wg_ref[0].astype(BF16), wu_ref[0].astype(BF16), wd_ref[0].astype(BF16)

    def normalise(sl):
        xn_ref[sl, :] = _rms(x_ref[sl, :], g_ref[...]).astype(BF16)

    def expert(sl, w):
        x = xn_ref[sl, :]
        a = (_silu(_dot(x, w[0])) * _dot(x, w[1])).astype(BF16)
        return _dot(a, w[2])

    dense = nvalid > tm - FFN_SUB

    @pl.when(dense & (j == 0))
    def _():
        w = weights()
        for sl in subs:
            normalise(sl)
            o_ref[sl, :] = expert(sl, w)

    @pl.when(dense & (j > 0))
    def _():
        w = weights()
        for sl in subs:
            o_ref[sl, :] += expert(sl, w)

    @pl.when(jnp.logical_not(dense))
    def _():
        w = weights()
        for sb, sl in enumerate(subs):
            @pl.when((sb * FFN_SUB < nvalid) & (j == 0))
            def _():
                normalise(sl)
                o_ref[sl, :] = expert(sl, w)

            @pl.when((sb * FFN_SUB < nvalid) & (j > 0))
            def _():
                o_ref[sl, :] += expert(sl, w)

            @pl.when((sb * FFN_SUB >= nvalid) & (j == 0))
            def _():
                o_ref[sl, :] = jnp.zeros((FFN_SUB, o_ref.shape[1]), F32)


def swiglu(x, g, wg, wu, wd, tile_expert, tile_valid, tm=FFN_TM, tf=FFN_TF):
    N, D = x.shape
    F = wg.shape[2]
    tm = min(tm, N)
    last = F // tf - 1

    def wj(i, j, nv):
        return jnp.where(nv[i] > 0, j, last)

    in_specs = [pl.BlockSpec((tm, D), lambda i, j, te, nv: (i, 0)),
                pl.BlockSpec((1, D), lambda i, j, te, nv: (0, 0)),
                pl.BlockSpec((1, D, tf), lambda i, j, te, nv: (te[i], 0, wj(i, j, nv))),
                pl.BlockSpec((1, D, tf), lambda i, j, te, nv: (te[i], 0, wj(i, j, nv))),
                pl.BlockSpec((1, tf, D), lambda i, j, te, nv: (te[i], wj(i, j, nv), 0))]
    return pl.pallas_call(
        functools.partial(_swiglu_kernel, tm=tm),
        grid_spec=pltpu.PrefetchScalarGridSpec(
            num_scalar_prefetch=2,
            grid=(N // tm, F // tf),
            in_specs=in_specs,
            out_specs=pl.BlockSpec((tm, D), lambda i, j, te, nv: (i, 0)),
            scratch_shapes=[pltpu.VMEM((tm, D), BF16)]),
        out_shape=jax.ShapeDtypeStruct((N, D), F32),
        compiler_params=_cparams(("arbitrary", "arbitrary"), VMEM_LIMIT),
        name="swiglu",
    )(tile_expert, tile_valid, x, g.reshape(1, D), wg, wu, wd)


def _route_kernel(lg_ref, info_ref, cnt_ref, run_ref, *, tb):
    i = pl.program_id(0)

    @pl.when(i == 0)
    def _():
        run_ref[...] = jnp.zeros_like(run_ref)

    lane = lax.broadcasted_iota(jnp.int32, (tb, LANES), 1)
    lg = jnp.where(lane < N_EXPERTS, lg_ref[...], -jnp.inf)
    m1 = jnp.max(lg, axis=1, keepdims=True)
    i1 = jnp.min(jnp.where(lg == m1, lane, LANES), axis=1, keepdims=True)
    lg2 = jnp.where(lane == i1, -jnp.inf, lg)
    m2 = jnp.max(lg2, axis=1, keepdims=True)
    i2 = jnp.min(jnp.where(lg2 == m2, lane, LANES), axis=1, keepdims=True)
    e2 = jnp.exp(m2 - m1)
    inv = 1.0 / (1.0 + e2)
    w1 = inv
    w2 = e2 * inv
    onehot = jnp.where((lane == i1) | (lane == i2), 1.0, 0.0)
    r = lax.broadcasted_iota(jnp.int32, (tb, tb), 0)
    c = lax.broadcasted_iota(jnp.int32, (tb, tb), 1)
    tri = jnp.where(c < r, 1.0, 0.0).astype(BF16)
    rank = _dot(tri, onehot.astype(BF16)) + run_ref[...]
    r1 = jnp.sum(jnp.where(lane == i1, rank, 0.0), axis=1, keepdims=True)
    r2 = jnp.sum(jnp.where(lane == i2, rank, 0.0), axis=1, keepdims=True)
    info = jnp.where(lane == 0, i1.astype(F32), 0.0)
    info = jnp.where(lane == 1, i2.astype(F32), info)
    info = jnp.where(lane == 2, r1, info)
    info = jnp.where(lane == 3, r2, info)
    info = jnp.where(lane == 4, w1, info)
    info = jnp.where(lane == 5, w2, info)
    info_ref[...] = info
    run_ref[...] = run_ref[...] + jnp.sum(onehot, axis=0, keepdims=True)
    cnt_ref[...] = run_ref[...]


def moe_route(logits, tb=512):
    S = logits.shape[0]
    return pl.pallas_call(
        functools.partial(_route_kernel, tb=tb),
        grid=(S // tb,),
        in_specs=[pl.BlockSpec((tb, LANES), lambda i: (i, 0))],
        out_specs=[pl.BlockSpec((tb, LANES), lambda i: (i, 0)),
                   pl.BlockSpec((1, LANES), lambda i: (0, 0))],
        out_shape=[jax.ShapeDtypeStruct((S, LANES), F32), jax.ShapeDtypeStruct((1, LANES), F32)],
        scratch_shapes=[pltpu.VMEM((1, LANES), F32)],
        compiler_params=_cparams(("arbitrary",)),
        name="moe_route",
    )(logits)


DISPATCH_TB = 256
ZERO_ROWS = 128
ZERO_REGIONS = N_EXPERTS + 1


def _dispatch_kernel(pos_ref, fill_ref, x_ref, o_ref, zero_ref, sem, zsem, *, tb, s):
    i = pl.program_id(0)

    def zero_region(first, n, wait):
        def piece(off, rows, aligned):
            if aligned:
                off = pl.multiple_of(off, SUBLANES)
            cp = pltpu.make_async_copy(zero_ref.at[pl.ds(0, rows), :],
                                       o_ref.at[pl.ds(off, rows), :], zsem)
            cp.wait() if wait else cp.start()

        head = jnp.minimum((-first) & (SUBLANES - 1), n)
        lax.fori_loop(0, head, lambda r, c: (piece(first + r, 1, False), c)[1], 0)
        base = first + head
        m = n - head
        big = ZERO_ROWS
        lax.fori_loop(0, m // big, lambda r, c: (piece(base + r * big, big, True), c)[1], 0)
        rem = m % big
        size = big // 2
        while size >= SUBLANES:
            @pl.when((rem & size) != 0)
            def _():
                piece(base + (m - rem) + (rem & ~(2 * size - 1)), size, True)
            size //= 2
        tail0 = base + (m & ~(SUBLANES - 1))
        lax.fori_loop(0, m & (SUBLANES - 1), lambda r, c: (piece(tail0 + r, 1, False), c)[1], 0)

    @pl.when(i == 0)
    def _():
        zero_ref[...] = jnp.zeros_like(zero_ref)
        for wait in (False, True):
            for g in range(ZERO_REGIONS):
                zero_region(fill_ref[g], fill_ref[ZERO_REGIONS + g], wait)

    def copy(r, k):
        dst = pos_ref[k * s + i * tb + r]
        return pltpu.make_async_copy(x_ref.at[pl.ds(r, 1), :], o_ref.at[pl.ds(dst, 1), :], sem)

    def start(r, carry):
        copy(r, 0).start()
        copy(r, 1).start()
        return carry

    lax.fori_loop(0, tb, start, 0)

    def drain(r, carry):
        copy(r, 0).wait()
        copy(r, 1).wait()
        return carry

    lax.fori_loop(0, tb, drain, 0)


def moe_dispatch(x, pos, fill, n_rows, tb=DISPATCH_TB):
    S, D = x.shape
    return pl.pallas_call(
        functools.partial(_dispatch_kernel, tb=tb, s=S),
        grid_spec=pltpu.PrefetchScalarGridSpec(
            num_scalar_prefetch=2,
            grid=(S // tb,),
            in_specs=[pl.BlockSpec((tb, D), lambda i, pos, fill: (i, 0))],
            out_specs=pl.BlockSpec(memory_space=pl.ANY),
            scratch_shapes=[pltpu.VMEM((ZERO_ROWS, D), x.dtype),
                            pltpu.SemaphoreType.DMA(()),
                            pltpu.SemaphoreType.DMA(())]),
        out_shape=jax.ShapeDtypeStruct((n_rows, D), x.dtype),
        compiler_params=_cparams(("arbitrary",)),
        name="moe_dispatch",
    )(pos, fill, x)


COMBINE_TB = 256


def _combine_kernel(pos_ref, h_ref, info_ref, g_ref, y_ref, o_ref, buf_ref, sem, *, tb, s):
    i = pl.program_id(0)

    def copy(r, k):
        src = pos_ref[k * s + i * tb + r]
        return pltpu.make_async_copy(y_ref.at[pl.ds(src, 1), :], buf_ref.at[k, pl.ds(r, 1), :], sem)

    def start(r, carry):
        copy(r, 0).start()
        copy(r, 1).start()
        return carry

    lax.fori_loop(0, tb, start, 0)

    def drain(r, carry):
        copy(r, 0).wait()
        copy(r, 1).wait()
        return carry

    lax.fori_loop(0, tb, drain, 0)
    w1 = info_ref[:, 4:5]
    w2 = info_ref[:, 5:6]
    first_lower = info_ref[:, 0:1] < info_ref[:, 1:2]
    ya = jnp.where(first_lower, w1 * buf_ref[0], w2 * buf_ref[1])
    yb = jnp.where(first_lower, w2 * buf_ref[1], w1 * buf_ref[0])
    o_ref[...] = _rms(h_ref[...] + (ya + yb), g_ref[...])


def moe_combine(h, info, pos, y_sorted, g, tb=COMBINE_TB):
    S, D = h.shape
    return pl.pallas_call(
        functools.partial(_combine_kernel, tb=tb, s=S),
        grid_spec=pltpu.PrefetchScalarGridSpec(
            num_scalar_prefetch=1,
            grid=(S // tb,),
            in_specs=[pl.BlockSpec((tb, D), lambda i, pos: (i, 0)),
                      pl.BlockSpec((tb, LANES), lambda i, pos: (i, 0)),
                      pl.BlockSpec((1, D), lambda i, pos: (0, 0)),
                      pl.BlockSpec(memory_space=pl.ANY)],
            out_specs=pl.BlockSpec((tb, D), lambda i, pos: (i, 0)),
            scratch_shapes=[pltpu.VMEM((2, tb, D), F32), pltpu.SemaphoreType.DMA(())]),
        out_shape=jax.ShapeDtypeStruct((S, D), F32),
        compiler_params=_cparams(("arbitrary",), VMEM_LIMIT),
        name="moe_combine",
    )(pos, h, info, g.reshape(1, D), y_sorted)


def moe_layer(h, norm_g, logits, wg, wu, wd, final_g, tm=FFN_TM):
    S, D = h.shape
    E = N_EXPERTS
    tm = min(tm, S)
    info, counts = moe_route(logits)
    cnt = counts[0, :E].astype(jnp.int32)
    padded = ((cnt + tm - 1) // tm) * tm
    ends = jnp.cumsum(padded)
    offs = ends - padded
    n_tiles = (2 * S) // tm + E
    e12 = info[:, 0:2].astype(jnp.int32)
    r12 = info[:, 2:4].astype(jnp.int32)
    pos = (jnp.take(offs, e12) + r12).T.reshape(2 * S)
    tile_start = jnp.arange(n_tiles, dtype=jnp.int32) * tm
    tile_expert = jnp.minimum(jnp.sum(tile_start[:, None] >= ends[None, :], axis=1), E - 1)
    tile_expert = tile_expert.astype(jnp.int32)
    tile_valid = jnp.clip(jnp.take(offs + cnt, tile_expert) - tile_start, 0, tm).astype(jnp.int32)
    n_rows = n_tiles * tm
    fill = jnp.concatenate([offs + cnt, ends[-1:], padded - cnt, n_rows - ends[-1:]])
    fill = fill.astype(jnp.int32)
    xs = moe_dispatch(h, pos, fill, n_tiles * tm)
    ys = swiglu(xs, norm_g, wg, wu, wd, tile_expert, tile_valid, tm=tm)
    return moe_combine(h, info, pos, ys, final_g)


def _final_norm_kernel(x_ref, g_ref, o_ref):
    o_ref[...] = _rms(x_ref[...], g_ref[...])


def final_norm(x, g, tm=512):
    S, D = x.shape
    return pl.pallas_call(
        _final_norm_kernel,
        grid=(S // tm,),
        in_specs=[pl.BlockSpec((tm, D), lambda i: (i, 0)), pl.BlockSpec((1, D), lambda i: (0, 0))],
        out_specs=pl.BlockSpec((tm, D), lambda i: (i, 0)),
        out_shape=jax.ShapeDtypeStruct((S, D), F32),
        compiler_params=_cparams(("arbitrary",), VMEM_LIMIT),
        name="final_norm",
    )(x, g.reshape(1, D))


def _pad_cols(w, n):
    return jnp.concatenate([w, jnp.zeros((w.shape[0], n - w.shape[1]), w.dtype)], axis=1)


def kernel(x, positions, norm_mix_g, w_in, conv_dw_w, conv_dw_b, conv_ln_g, conv_ln_b, conv_pw_w, conv_pw_b, nsa_gate_b, nsa_pe_k, nsa_pe_v, nsa_w1k, nsa_w2k, nsa_w1v, nsa_w2v, ssm_lambda_re, ssm_lambda_im, ssm_log_dt, ssm_b_re, ssm_b_im, ssm_c_re, ssm_c_im, ssm_d, ssm_glu_w, ssm_glu_b, mix_out_g, w_out, norm_ffn_g, ffn_w_gate, ffn_w_up, ffn_w_down, router_w, moe_w_gate, moe_w_up, moe_w_down, final_norm_g):
    B, S, D = x.shape
    depth = w_in.shape[0]
    outs = []
    for b in range(B):
        h = x[b]
        pos = positions[b]
        pending = None
        out = None
        for l in range(depth):
            w_main = jnp.concatenate([w_in[l][:, :OFF_GATE], w_in[l][:, OFF_SSM:]], axis=1)
            w_gate = _pad_cols(w_in[l][:, OFF_GATE:OFF_SSM], LANES)
            if pending is None:
                hn, gate_logits = norm_side(h, norm_mix_g[l], w_gate, exact_side=False)
            else:
                h, hn, gate_logits = norm_side(h, norm_mix_g[l], w_gate, exact_side=False,
                                               add=pending)
                pending = None
            proj = matmul(hn, w_main)
            g_mix = mix_out_g[l]
            conv_o = conv_module(proj, conv_dw_w[l], conv_dw_b[l], conv_ln_g[l], conv_ln_b[l],
                                 conv_pw_w[l], conv_pw_b[l], g_mix[:D_CONV])
            q_raw, q_rope_t, cv_raw, ks_aug, vs_t, kw, vw_t, gates = attn_prep(
                proj, gate_logits, nsa_gate_b[l], pos)
            kc, kct = nsa_compress(cv_raw, jnp.stack([nsa_w1k[l], nsa_w1v[l]]),
                                   jnp.stack([nsa_w2k[l], nsa_w2v[l]]),
                                   jnp.stack([nsa_pe_k[l], nsa_pe_v[l]]))
            o_c, mask = nsa_compressed(q_raw, kc, kct)
            ns = mask.shape[1]
            if ns % LANES:
                mask = jnp.pad(mask, ((0, 0), (0, LANES - ns % LANES), (0, 0)),
                               constant_values=1.0)
            o_s = nsa_selected(q_rope_t, mask, ks_aug, vs_t)
            o_w = nsa_window(q_rope_t, kw, vw_t)
            att_o = nsa_gate(o_c, o_s, o_w, gates, g_mix[D_CONV:D_CONV + D_ATT])
            ssm_o = ssm_mixer(proj, ssm_lambda_re[l], ssm_lambda_im[l], ssm_log_dt[l],
                              ssm_b_re[l], ssm_b_im[l], ssm_c_re[l], ssm_c_im[l], ssm_d[l],
                              ssm_glu_w[l], ssm_glu_b[l], g_mix[D_CONV + D_ATT:])
            mixed = jnp.concatenate([conv_o, att_o, ssm_o], axis=1)
            h = matmul(mixed, w_out[l], res=h)
            i = l // 2
            if l % 2 == 0:
                n_t = S // min(FFN_TM, S)
                pending = swiglu(h, norm_ffn_g[l], ffn_w_gate[i:i + 1], ffn_w_up[i:i + 1],
                                 ffn_w_down[i:i + 1], jnp.zeros((n_t,), jnp.int32),
                                 jnp.full((n_t,), min(FFN_TM, S), jnp.int32))
            else:
                if l != depth - 1:
                    raise NotImplementedError("a MoE layer is only supported as the last layer")
                (logits,) = norm_side(h, norm_ffn_g[l], _pad_cols(router_w[i], LANES),
                                      exact_side=True, emit_hn=False)
                out = moe_layer(h, norm_ffn_g[l], logits, moe_w_gate[i], moe_w_up[i],
                                moe_w_down[i], final_norm_g)
        if out is None:
            if pending is not None:
                h = h + pending
            out = final_norm(h, final_norm_g)
        outs.append(out)
    return jnp.stack(outs)
```

```python
import functools
import math

import jax
import jax.numpy as jnp
from jax import lax
from jax.experimental import pallas as pl
from jax.experimental.pallas import tpu as pltpu

F32 = jnp.float32
BF16 = jnp.bfloat16

D_MODEL = 2048
D_CONV = 512
D_ATT = 1024
D_SSM = 512
CONV_WIDTH = 31
HEAD_DIM = 128
N_HEADS = 8
N_KV_HEADS = 2
GQA = 4
ROT_DIM = 32
ROPE_THETA = 500000.0
CMP_LEN = 32
CMP_STRIDE = 16
SLC_LEN = 64
SLC_SHIFT = 6
N_SEL = 16
WINDOW = 512
FORCE_BONUS = 1000.0
SSM_GROUP = 16
N_SSM_GROUPS = 32
SSM_STATE = 64
N_EXPERTS = 8
EPS = 1e-5

OFF_Q = 2 * D_CONV
OFF_KV = OFF_Q + D_ATT
OFF_GATE = OFF_KV + 6 * N_KV_HEADS * HEAD_DIM
OFF_SSM = OFF_GATE + 3 * N_HEADS
D_MAIN = 4096

LANES = 128
SUBLANES = 8
MASK_NEG = -1e30
BIAS_NEG = -2e30
LOG2E = math.log2(math.e)
VMEM_LIMIT = 56 * 1024 * 1024

SSM_N = N_SSM_GROUPS * SSM_STATE


def _cparams(sem, vmem=None):
    return pltpu.CompilerParams(dimension_semantics=sem, vmem_limit_bytes=vmem)


def _sigmoid(x):
    return 1.0 / (1.0 + jnp.exp(-x))


def _silu(x):
    return x * _sigmoid(x)


def _gelu_tanh(x):
    return 0.5 * x * (1.0 + jnp.tanh(math.sqrt(2.0 / math.pi) * (x + 0.044715 * (x * x * x))))


def _rms(x, g):
    return x * lax.rsqrt(jnp.mean(x * x, axis=-1, keepdims=True) + EPS) * g


def _dot(a, b):
    return jnp.dot(a, b, preferred_element_type=F32)


def _dot_nt(a, b):
    return lax.dot_general(a, b, (((1,), (1,)), ((), ())), preferred_element_type=F32)


def _norm_side_kernel(*refs, exact_side, has_add, emit_hn):
    refs = list(refs)
    x = refs.pop(0)[...]
    if has_add:
        x = x + refs.pop(0)[...]
    g_ref, w_ref = refs.pop(0), refs.pop(0)
    if has_add:
        refs.pop(0)[...] = x
    y = _rms(x, g_ref[...])
    if emit_hn:
        refs.pop(0)[...] = y.astype(BF16)
    (side_ref,) = refs
    if exact_side:
        side_ref[...] = jnp.dot(y, w_ref[...], preferred_element_type=F32,
                                precision=lax.Precision.HIGHEST)
    else:
        side_ref[...] = _dot(y.astype(BF16), w_ref[...].astype(BF16))


def norm_side(x, g, w_side, exact_side, add=None, emit_hn=True, tm=512):
    S, D = x.shape
    has_add = add is not None
    blk = pl.BlockSpec((tm, D), lambda i: (i, 0))
    in_specs = [blk] + ([blk] if has_add else []) + [pl.BlockSpec((1, D), lambda i: (0, 0)),
                                                     pl.BlockSpec((D, LANES), lambda i: (0, 0))]
    out_specs = (([blk] if has_add else []) + ([blk] if emit_hn else [])
                 + [pl.BlockSpec((tm, LANES), lambda i: (i, 0))])
    out_shape = (([jax.ShapeDtypeStruct((S, D), F32)] if has_add else [])
                 + ([jax.ShapeDtypeStruct((S, D), BF16)] if emit_hn else [])
                 + [jax.ShapeDtypeStruct((S, LANES), F32)])
    args = [x] + ([add] if has_add else []) + [g.reshape(1, D), w_side]
    return pl.pallas_call(
        functools.partial(_norm_side_kernel, exact_side=exact_side, has_add=has_add,
                          emit_hn=emit_hn),
        grid=(S // tm,),
        in_specs=in_specs,
        out_specs=out_specs,
        out_shape=out_shape,
        compiler_params=_cparams(("arbitrary",), VMEM_LIMIT),
        name="norm_side",
    )(*args)


def _mm_kernel(a_ref, w_ref, o_ref):
    o_ref[...] = _dot(a_ref[...], w_ref[...].astype(BF16))


def _mm_res_kernel(a_ref, w_ref, r_ref, o_ref):
    o_ref[...] = r_ref[...] + _dot(a_ref[...], w_ref[...].astype(BF16))


def matmul(a, w, res=None, tm=1024, tn=512):
    M, K = a.shape
    N = w.shape[1]
    tm = min(tm, M)
    in_specs = [pl.BlockSpec((tm, K), lambda j, i: (i, 0)),
                pl.BlockSpec((K, tn), lambda j, i: (0, j))]
    args = [a, w]
    kern = _mm_kernel
    if res is not None:
        in_specs.append(pl.BlockSpec((tm, tn), lambda j, i: (i, j)))
        args.append(res)
        kern = _mm_res_kernel
    return pl.pallas_call(
        kern,
        grid=(N // tn, M // tm),
        in_specs=in_specs,
        out_specs=pl.BlockSpec((tm, tn), lambda j, i: (i, j)),
        out_shape=jax.ShapeDtypeStruct((M, N), F32),
        compiler_params=_cparams(("arbitrary", "arbitrary"), VMEM_LIMIT),
        name="matmul",
    )(*args)


CONV_HALO = 32
CONV_RC = 32


def _conv_kernel(a_ref, b_ref, ah_ref, bh_ref, dww_ref, dwb_ref, lng_ref, lnb_ref, pww_ref,
                 pwb_ref, g_ref, o_ref, u_ref, us_ref, y_ref, *, ts):
    i = pl.program_id(0)
    uh = ah_ref[...] * _sigmoid(bh_ref[...])
    u_ref[0:CONV_HALO, :] = jnp.where(i > 0, uh, 0.0)
    u_ref[CONV_HALO:, :] = a_ref[...] * _sigmoid(b_ref[...])
    lead = CONV_HALO - (CONV_WIDTH - 1)
    span = ts + CONV_HALO - SUBLANES
    for ph in range(1, SUBLANES):
        us_ref[ph - 1] = u_ref[ph:ph + span, :]

    def tap(c, k):
        ph = (lead + k) % SUBLANES
        r0 = c * CONV_RC + (lead + k) - ph
        if ph == 0:
            return u_ref[r0:r0 + CONV_RC, :]
        return us_ref[ph - 1, r0:r0 + CONV_RC, :]

    for c in range(ts // CONV_RC):
        acc = dww_ref[0:1, :] * tap(c, 0)
        for k in range(1, CONV_WIDTH):
            acc = acc + dww_ref[k:k + 1, :] * tap(c, k)
        y_ref[c * CONV_RC:(c + 1) * CONV_RC, :] = acc + dwb_ref[...]
    y = y_ref[...]
    mu = jnp.mean(y, axis=-1, keepdims=True)
    yc = y - mu
    var = jnp.mean(yc * yc, axis=-1, keepdims=True)
    z = _silu(yc * lax.rsqrt(var + EPS) * lng_ref[...] + lnb_ref[...])
    o = _dot(z.astype(BF16), pww_ref[...].astype(BF16)) + pwb_ref[...]
    o_ref[...] = _rms(o, g_ref[...]).astype(BF16)


def conv_module(proj, dw_w, dw_b, ln_g, ln_b, pw_w, pw_b, g, ts=512):
    S = proj.shape[0]
    C = D_CONV
    hb = ts // CONV_HALO
    row = lambda v: v.reshape(1, C)
    vec = pl.BlockSpec((1, C), lambda i: (0, 0))
    halo = lambda col: pl.BlockSpec((CONV_HALO, C), lambda i: (jnp.maximum(i * hb - 1, 0), col))
    return pl.pallas_call(
        functools.partial(_conv_kernel, ts=ts),
        grid=(S // ts,),
        in_specs=[pl.BlockSpec((ts, C), lambda i: (i, 0)),
                  pl.BlockSpec((ts, C), lambda i: (i, 1)),
                  halo(0), halo(1),
                  pl.BlockSpec((CONV_WIDTH, C), lambda i: (0, 0)),
                  vec, vec, vec,
                  pl.BlockSpec((C, C), lambda i: (0, 0)),
                  vec, vec],
        out_specs=pl.BlockSpec((ts, C), lambda i: (i, 0)),
        out_shape=jax.ShapeDtypeStruct((S, C), BF16),
        scratch_shapes=[pltpu.VMEM((ts + CONV_HALO, C), F32),
                        pltpu.VMEM((SUBLANES - 1, ts + CONV_HALO - SUBLANES, C), F32),
                        pltpu.VMEM((ts, C), F32)],
        compiler_params=_cparams(("arbitrary",), VMEM_LIMIT),
        name="conv_module",
    )(proj, proj, proj, proj, dw_w, row(dw_b), row(ln_g), row(ln_b), pw_w, row(pw_b), row(g))


def _prep_kernel(q_ref, kva_ref, kvb_ref, kvc_ref, gl_ref, gb_ref, pos_ref, inv_ref, sgn_ref,
                 qraw_ref, qropet_ref, cv_ref, ks_ref, vst_ref, kw_ref, vwt_ref,
                 gate_ref, *, ts):
    i = pl.program_id(0)
    ang = pos_ref[...].astype(F32) * inv_ref[...]
    cos = jnp.cos(ang)
    sin = jnp.sin(ang) * sgn_ref[...]
    lane = lax.broadcasted_iota(jnp.int32, (ts, LANES), 1)
    half = ROT_DIM // 2

    def rope(x):
        sw = jnp.where(lane < half, pltpu.roll(x, LANES - half, 1), pltpu.roll(x, half, 1))
        return x * cos + sw * sin

    scale = HEAD_DIM ** -0.5
    for h in range(N_HEADS):
        q = q_ref[:, h * HEAD_DIM:(h + 1) * HEAD_DIM]
        qraw_ref[h] = (q * scale).astype(BF16)
        qropet_ref[h] = (rope(q) * (scale * LOG2E)).T.astype(BF16)
    key = i * ts + lax.broadcasted_iota(jnp.int32, (ts, LANES), 0)
    bias = jnp.where(lane == ((key >> SLC_SHIFT) & (LANES - 1)), BIAS_NEG, 0.0).astype(BF16)
    for g in range(N_KV_HEADS):
        lo = g * HEAD_DIM
        hi = (N_KV_HEADS + g) * HEAD_DIM
        cv_ref[g] = kva_ref[:, lo:lo + HEAD_DIM].astype(BF16)
        cv_ref[N_KV_HEADS + g] = kva_ref[:, hi:hi + HEAD_DIM].astype(BF16)
        ks_ref[g, :, 0:HEAD_DIM] = rope(kvb_ref[:, lo:lo + HEAD_DIM]).astype(BF16)
        ks_ref[g, :, HEAD_DIM:2 * HEAD_DIM] = bias
        vst_ref[g] = kvb_ref[:, hi:hi + HEAD_DIM].T.astype(BF16)
        kw_ref[g] = rope(kvc_ref[:, lo:lo + HEAD_DIM]).astype(BF16)
        vwt_ref[g] = kvc_ref[:, hi:hi + HEAD_DIM].T.astype(BF16)
    gate_ref[...] = _sigmoid(gl_ref[...] + gb_ref[...])


def attn_prep(proj, gate_logits, gate_b, positions, ts=256):
    S = proj.shape[0]
    G = N_KV_HEADS
    inv = ROPE_THETA ** (-jnp.arange(0, ROT_DIM, 2, dtype=F32) / ROT_DIM)
    inv_full = jnp.concatenate([inv, inv, jnp.zeros((LANES - ROT_DIM,), F32)]).reshape(1, LANES)
    half = ROT_DIM // 2
    sgn = jnp.concatenate([-jnp.ones((half,), F32), jnp.ones((half,), F32),
                           jnp.zeros((LANES - ROT_DIM,), F32)]).reshape(1, LANES)
    gb = jnp.concatenate([gate_b, jnp.zeros((LANES - gate_b.shape[0],), F32)]).reshape(1, LANES)
    kvw = 2 * G * HEAD_DIM
    kv0 = OFF_KV // kvw
    row = pl.BlockSpec((1, LANES), lambda i: (0, 0))
    hm = lambda n, w: pl.BlockSpec((n, ts, w), lambda i: (0, i, 0))
    hmt = lambda n: pl.BlockSpec((n, HEAD_DIM, ts), lambda i: (0, 0, i))
    return pl.pallas_call(
        functools.partial(_prep_kernel, ts=ts),
        grid=(S // ts,),
        in_specs=[pl.BlockSpec((ts, D_ATT), lambda i: (i, OFF_Q // D_ATT)),
                  pl.BlockSpec((ts, kvw), lambda i: (i, kv0)),
                  pl.BlockSpec((ts, kvw), lambda i: (i, kv0 + 1)),
                  pl.BlockSpec((ts, kvw), lambda i: (i, kv0 + 2)),
                  pl.BlockSpec((ts, LANES), lambda i: (i, 0)),
                  row,
                  pl.BlockSpec((ts, 1), lambda i: (i, 0)),
                  row, row],
        out_specs=[hm(N_HEADS, HEAD_DIM), hmt(N_HEADS), hm(2 * G, HEAD_DIM),
                   hm(G, 2 * HEAD_DIM), hmt(G), hm(G, HEAD_DIM), hmt(G),
                   pl.BlockSpec((ts, LANES), lambda i: (i, 0))],
        out_shape=[jax.ShapeDtypeStruct((N_HEADS, S, HEAD_DIM), BF16),
                   jax.ShapeDtypeStruct((N_HEADS, HEAD_DIM, S), BF16),
                   jax.ShapeDtypeStruct((2 * G, S, HEAD_DIM), BF16),
                   jax.ShapeDtypeStruct((G, S, 2 * HEAD_DIM), BF16),
                   jax.ShapeDtypeStruct((G, HEAD_DIM, S), BF16),
                   jax.ShapeDtypeStruct((G, S, HEAD_DIM), BF16),
                   jax.ShapeDtypeStruct((G, HEAD_DIM, S), BF16),
                   jax.ShapeDtypeStruct((S, LANES), F32)],
        compiler_params=_cparams(("arbitrary",), VMEM_LIMIT),
        name="attn_prep",
    )(proj, proj, proj, proj, gate_logits, gb, positions.reshape(S, 1), inv_full, sgn)


def _compress_kernel(x_ref, w1_ref, w2_ref, pe_ref, o_ref, ot_ref, *, nch):
    x = x_ref[0]
    w1a = w1_ref[0, 0].astype(BF16)
    w1b = w1_ref[0, 1].astype(BF16)
    first = _dot(x, w1a)
    second = _dot(x, w1b)
    pe_a = jnp.broadcast_to(pe_ref[0, 0], (8, x.shape[1])).astype(BF16)
    pe_b = jnp.broadcast_to(pe_ref[0, 1], (8, x.shape[1])).astype(BF16)
    c0 = (_dot(pe_a, w1a) + _dot(pe_b, w1b))[0:1, :]
    hid = _silu(first + pltpu.roll(second, nch - 1, 0) + c0)
    out = _dot(hid.astype(BF16), w2_ref[0].astype(BF16))
    rowi = lax.broadcasted_iota(jnp.int32, out.shape, 0)
    out = jnp.where(rowi < nch - 1, out, 0.0)
    o_ref[0] = out.astype(BF16)
    ot_ref[0] = out.T.astype(BF16)


def nsa_compress(cv_raw, w1, w2, pe):
    n4, S, hd = cv_raw.shape
    G = N_KV_HEADS
    nch = S // CMP_STRIDE
    cw = CMP_STRIDE * hd
    x = cv_raw.reshape(n4, nch, cw)
    w1r = w1.reshape(2, 2, cw, hd)
    per = pe.reshape(2, 2, 1, cw)
    return pl.pallas_call(
        functools.partial(_compress_kernel, nch=nch),
        grid=(n4,),
        in_specs=[pl.BlockSpec((1, nch, cw), lambda i: (i, 0, 0)),
                  pl.BlockSpec((1, 2, cw, hd), lambda i: (i // G, 0, 0, 0)),
                  pl.BlockSpec((1, hd, hd), lambda i: (i // G, 0, 0)),
                  pl.BlockSpec((1, 2, 1, cw), lambda i: (i // G, 0, 0, 0))],
        out_specs=[pl.BlockSpec((1, nch, hd), lambda i: (i, 0, 0)),
                   pl.BlockSpec((1, hd, nch), lambda i: (i, 0, 0))],
        out_shape=[jax.ShapeDtypeStruct((n4, nch, hd), BF16),
                   jax.ShapeDtypeStruct((n4, hd, nch), BF16)],
        compiler_params=_cparams(("arbitrary",), VMEM_LIMIT),
        name="nsa_compress",
    )(x, w1r, w2, per)


CMP_PAD = 8
CMP_CLASS = 256


def _cmp_kernel(q_ref, kc_ref, vct_ref, oc_ref, mask_ref, pad_ref, *, tq, nch, ns, n_sel):
    qb = pl.program_id(1)
    q0 = qb * tq
    t = q0 + lax.broadcasted_iota(jnp.int32, (1, tq), 1)
    ratio = SLC_LEN // CMP_STRIDE

    def run(nr):
        nb = nr // ratio
        n_idx = lax.broadcasted_iota(jnp.int32, (nr, tq), 0)
        visible = (n_idx * CMP_STRIDE + (CMP_LEN - 1)) <= t
        kc = kc_ref[0, 0:nr, :]
        vct = vct_ref[0, :, 0:nr]
        psum = jnp.zeros((nr, tq), F32)
        for r in range(GQA):
            s = jnp.where(visible, _dot_nt(kc, q_ref[r]), MASK_NEG)
            m = jnp.max(s, axis=0, keepdims=True)
            p = jnp.where(visible, jnp.exp(s - m), 0.0)
            denom = jnp.maximum(jnp.sum(p, axis=0, keepdims=True), 1e-30)
            p = p * (1.0 / denom)
            psum = psum + p
            ot = _dot(vct, p.astype(BF16))
            oc_ref[:, r * HEAD_DIM:(r + 1) * HEAD_DIM] = ot.T
        pad_ref[0:CMP_PAD, :] = jnp.zeros((CMP_PAD, tq), F32)
        pad_ref[CMP_PAD:CMP_PAD + nr, :] = psum
        imp = pad_ref[pl.ds(CMP_PAD - 1, nb, stride=ratio), :]
        for r in range(ratio):
            imp = imp + pad_ref[pl.ds(CMP_PAD + r, nb, stride=ratio), :]
        j = lax.broadcasted_iota(jnp.int32, (nb, tq), 0)
        cur = t >> SLC_SHIFT
        allowed = j <= cur
        forced = (j == 0) | (j == cur) | (j == cur - 1)
        score = jnp.where(forced, imp + FORCE_BONUS, imp)
        score = jnp.where(allowed, score, -1.0)
        picked = jnp.zeros((nb, tq), F32)
        for _ in range(min(n_sel, nb)):
            mx = jnp.max(score, axis=0, keepdims=True)
            first = jnp.min(jnp.where(score == mx, j, nb), axis=0, keepdims=True)
            hit = j == first
            picked = jnp.where(hit, 1.0, picked)
            score = jnp.where(hit, -jnp.inf, score)
        dropped = jnp.where(allowed, 1.0 - picked, 1.0)
        mask_ref[0, 0:nb, :] = dropped.astype(BF16)
        if nb < ns:
            mask_ref[0, nb:ns, :] = jnp.ones((ns - nb, tq), BF16)

    need = (q0 + tq) // CMP_STRIDE
    nclass = -(-nch // CMP_CLASS)
    for c in range(nclass):
        @pl.when((need - 1) // CMP_CLASS == c)
        def _():
            run(min((c + 1) * CMP_CLASS, nch))


def nsa_compressed(q_raw, kc, vct, tq=128):
    _, S, hd = q_raw.shape
    G = N_KV_HEADS
    nch = S // CMP_STRIDE
    ns = S // SLC_LEN
    n_sel = min(N_SEL, ns)
    return pl.pallas_call(
        functools.partial(_cmp_kernel, tq=tq, nch=nch, ns=ns, n_sel=n_sel),
        grid=(G, S // tq),
        in_specs=[pl.BlockSpec((GQA, tq, hd), lambda g, i: (g, i, 0)),
                  pl.BlockSpec((1, nch, hd), lambda g, i: (g, 0, 0)),
                  pl.BlockSpec((1, hd, nch), lambda g, i: (G + g, 0, 0))],
        out_specs=[pl.BlockSpec((tq, GQA * hd), lambda g, i: (i, g)),
                   pl.BlockSpec((1, ns, tq), lambda g, i: (g, 0, i))],
        out_shape=[jax.ShapeDtypeStruct((S, D_ATT), F32),
                   jax.ShapeDtypeStruct((G, ns, S), BF16)],
        scratch_shapes=[pltpu.VMEM((nch + CMP_PAD, tq), F32)],
        compiler_params=_cparams(("arbitrary", "arbitrary"), VMEM_LIMIT),
        name="nsa_compressed",
    )(q_raw, kc, vct)


SEL_TK = 512
SEL_TQ = 128


def _sel_kernel(qt_ref, mask_ref, k_ref, vt_ref, o_ref, qa_ref, s0_ref, s1_ref, p0_ref, p1_ref,
                m_ref, l_ref, acc_ref, *, tq, tk, nparts):
    qb = pl.program_id(1)
    q0 = qb * tq
    cols = GQA * tq
    hd = HEAD_DIM
    last = (q0 + tq - 1) // tk
    tiles_per_part = (LANES * SLC_LEN) // tk
    for part in range(nparts):
        flags = mask_ref[0, part * LANES:(part + 1) * LANES, :]
        for r in range(GQA):
            qa_ref[part, 0:hd, r * tq:(r + 1) * tq] = qt_ref[r]
            qa_ref[part, hd:hd + LANES, r * tq:(r + 1) * tq] = flags
    m_ref[...] = jnp.full((1, cols), MASK_NEG, F32)
    l_ref[...] = jnp.zeros((1, cols), F32)
    acc_ref[...] = jnp.zeros((hd, cols), F32)
    s_bufs = (s0_ref, s1_ref)
    p_bufs = (p0_ref, p1_ref)
    p1_ref[...] = jnp.zeros((tk, cols), BF16)

    def scores(kt):
        k0 = pl.multiple_of(kt * tk, tk)
        return _dot(k_ref[0, pl.ds(k0, tk), :], qa_ref[kt // tiles_per_part])

    def values(kt, p):
        k0 = pl.multiple_of(kt * tk, tk)
        return _dot(vt_ref[0, :, pl.ds(k0, tk)], p)

    def softmax_step(s_buf, p_buf, pv_prev, keep=None):
        sub = 8
        rows = lambda r: slice(r * sub, (r + 1) * sub)
        tile_rows = lambda r: s_buf[rows(r), :] if keep is None else jnp.where(
            keep[rows(r), :], s_buf[rows(r), :], MASK_NEG)
        mx = tile_rows(0)
        for r in range(1, tk // sub):
            mx = jnp.maximum(mx, tile_rows(r))
        m_old = m_ref[...]
        m_new = jnp.maximum(m_old, jnp.max(mx, axis=0, keepdims=True))
        alpha = jnp.exp2(m_old - m_new)
        m_rows = jnp.broadcast_to(m_new, (sub, cols))
        tot = jnp.zeros((sub, cols), F32)
        for r in range(0, tk // sub, 2):
            pa = jnp.exp2(tile_rows(r) - m_rows)
            pb = jnp.exp2(tile_rows(r + 1) - m_rows)
            tot = tot + (pa + pb)
            p_buf[r * sub:(r + 2) * sub, :] = jnp.concatenate([pa, pb], axis=0).astype(BF16)
        l_ref[...] = alpha * l_ref[...] + jnp.sum(tot, axis=0, keepdims=True)
        acc_ref[...] = alpha * (acc_ref[...] + pv_prev)
        m_ref[...] = m_new

    s0_ref[...] = scores(0)

    def step(kt, cur):
        pv_prev = values(jnp.maximum(kt - 1, 0), p_bufs[1 - cur][...])
        s_bufs[1 - cur][...] = scores(kt + 1)
        softmax_step(s_bufs[cur], p_bufs[cur], pv_prev)

    def pair(i, carry):
        step(2 * i, 0)
        step(2 * i + 1, 1)
        return carry

    lax.fori_loop(0, last // 2, pair, 0)

    def finish(cur):
        pv_prev = values(jnp.maximum(last - 1, 0), p_bufs[1 - cur][...])
        key = last * tk + lax.broadcasted_iota(jnp.int32, (tk, 1), 0)
        t_col = q0 + (lax.broadcasted_iota(jnp.int32, (1, cols), 1) & (tq - 1))
        softmax_step(s_bufs[cur], p_bufs[cur], pv_prev, keep=key <= t_col)
        o = (acc_ref[...] + values(last, p_bufs[cur][...])) * (1.0 / l_ref[...])
        for r in range(GQA):
            o_ref[:, r * hd:(r + 1) * hd] = o[:, r * tq:(r + 1) * tq].T

    @pl.when((last & 1) == 1)
    def _():
        step(last - 1, 0)
        finish(1)

    @pl.when((last & 1) == 0)
    def _():
        finish(0)


def nsa_selected(q_rope_t, mask, ks_aug, vs_t, tq=SEL_TQ, tk=SEL_TK):
    _, hd, S = q_rope_t.shape
    G = N_KV_HEADS
    tk = min(tk, S)
    nsp = mask.shape[1]
    nparts = nsp // LANES
    cols = GQA * tq
    return pl.pallas_call(
        functools.partial(_sel_kernel, tq=tq, tk=tk, nparts=nparts),
        grid=(G, S // tq),
        in_specs=[pl.BlockSpec((GQA, hd, tq), lambda g, i: (g, 0, i)),
                  pl.BlockSpec((1, nsp, tq), lambda g, i: (g, 0, i)),
                  pl.BlockSpec((1, S, hd + LANES), lambda g, i: (g, 0, 0)),
                  pl.BlockSpec((1, hd, S), lambda g, i: (g, 0, 0))],
        out_specs=pl.BlockSpec((tq, GQA * hd), lambda g, i: (i, g)),
        out_shape=jax.ShapeDtypeStruct((S, D_ATT), F32),
        scratch_shapes=[pltpu.VMEM((nparts, hd + LANES, cols), BF16),
                        pltpu.VMEM((tk, cols), F32), pltpu.VMEM((tk, cols), F32),
                        pltpu.VMEM((tk, cols), BF16), pltpu.VMEM((tk, cols), BF16),
                        pltpu.VMEM((1, cols), F32),
                        pltpu.VMEM((1, cols), F32),
                        pltpu.VMEM((hd, cols), F32)],
        compiler_params=_cparams(("arbitrary", "arbitrary"), VMEM_LIMIT),
        name="nsa_selected",
    )(q_rope_t, mask, ks_aug, vs_t)


def _win_kernel(qt_ref, k_ref, vt_ref, o_ref, qa_ref, s_ref, p_ref, *, tq, span):
    qb = pl.program_id(1)
    q0 = qb * tq
    cols = GQA * tq
    hd = HEAD_DIM
    sub = 8
    nsl = span // sub
    rows = lambda r: slice(r * sub, (r + 1) * sub)
    start = pl.multiple_of(jnp.maximum(q0 + tq - span, 0), tq)
    for r in range(GQA):
        qa_ref[:, r * tq:(r + 1) * tq] = qt_ref[r]
    s_ref[...] = _dot(k_ref[0, pl.ds(start, span), :], qa_ref[...])
    t_col = q0 + (lax.broadcasted_iota(jnp.int32, (1, cols), 1) & (tq - 1))

    def attend(masked):
        for r in masked:
            diff = t_col - (start + r * sub + lax.broadcasted_iota(jnp.int32, (sub, 1), 0))
            s_ref[rows(r), :] = jnp.where((diff >= 0) & (diff < WINDOW), s_ref[rows(r), :],
                                          MASK_NEG)
        mx = s_ref[rows(0), :]
        for r in range(1, nsl):
            mx = jnp.maximum(mx, s_ref[rows(r), :])
        m_rows = jnp.broadcast_to(jnp.max(mx, axis=0, keepdims=True), (sub, cols))
        tot = jnp.zeros((sub, cols), F32)
        for r in range(0, nsl, 2):
            pa = jnp.exp2(s_ref[rows(r), :] - m_rows)
            pb = jnp.exp2(s_ref[rows(r + 1), :] - m_rows)
            tot = tot + (pa + pb)
            p_ref[r * sub:(r + 2) * sub, :] = jnp.concatenate([pa, pb], axis=0).astype(BF16)
        l = jnp.sum(tot, axis=0, keepdims=True)
        o = _dot(vt_ref[0, :, pl.ds(start, span)], p_ref[...]) * (1.0 / l)
        for r in range(GQA):
            o_ref[:, r * hd:(r + 1) * hd] = o[:, r * tq:(r + 1) * tq].T

    edge = tq // sub
    interior = q0 + tq - span >= 0

    @pl.when(interior)
    def _():
        attend(list(range(edge)) + list(range(nsl - edge, nsl)))

    @pl.when(jnp.logical_not(interior))
    def _():
        attend(list(range(nsl)))


def nsa_window(q_rope_t, kw, vw_t, tq=128):
    _, hd, S = q_rope_t.shape
    G = N_KV_HEADS
    span = min(WINDOW + tq, S)
    cols = GQA * tq
    return pl.pallas_call(
        functools.partial(_win_kernel, tq=tq, span=span),
        grid=(G, S // tq),
        in_specs=[pl.BlockSpec((GQA, hd, tq), lambda g, i: (g, 0, i)),
                  pl.BlockSpec((1, S, hd), lambda g, i: (g, 0, 0)),
                  pl.BlockSpec((1, hd, S), lambda g, i: (g, 0, 0))],
        out_specs=pl.BlockSpec((tq, GQA * hd), lambda g, i: (i, g)),
        out_shape=jax.ShapeDtypeStruct((S, D_ATT), F32),
        scratch_shapes=[pltpu.VMEM((hd, cols), BF16),
                        pltpu.VMEM((span, cols), F32),
                        pltpu.VMEM((span, cols), BF16)],
        compiler_params=_cparams(("arbitrary", "arbitrary"), VMEM_LIMIT),
        name="nsa_window",
    )(q_rope_t, kw, vw_t)


def _gate_kernel(oc_ref, os_ref, ow_ref, gate_ref, g_ref, o_ref, acc_ref):
    hd = HEAD_DIM
    for h in range(N_HEADS):
        sl = slice(h * hd, (h + 1) * hd)
        acc_ref[:, sl] = (gate_ref[:, 3 * h:3 * h + 1] * oc_ref[:, sl]
                          + gate_ref[:, 3 * h + 1:3 * h + 2] * os_ref[:, sl]
                          + gate_ref[:, 3 * h + 2:3 * h + 3] * ow_ref[:, sl])
    o_ref[...] = _rms(acc_ref[...], g_ref[...]).astype(BF16)


def nsa_gate(o_c, o_s, o_w, gates, g, ts=512):
    S = o_c.shape[0]
    blk = pl.BlockSpec((ts, D_ATT), lambda i: (i, 0))
    return pl.pallas_call(
        _gate_kernel,
        grid=(S // ts,),
        in_specs=[blk, blk, blk,
                  pl.BlockSpec((ts, LANES), lambda i: (i, 0)),
                  pl.BlockSpec((1, D_ATT), lambda i: (0, 0))],
        out_specs=blk,
        out_shape=jax.ShapeDtypeStruct((S, D_ATT), BF16),
        scratch_shapes=[pltpu.VMEM((ts, D_ATT), F32)],
        compiler_params=_cparams(("arbitrary",), VMEM_LIMIT),
        name="nsa_gate",
    )(o_c, o_s, o_w, gates, g.reshape(1, D_ATT))


SSM_ROWS = 8
def _ssm_kernel(u_ref, wbr_ref, wbi_ref, wcr_ref, wci_ref, stepr_ref, stepi_ref, powr_ref,
                powi_ref, d_ref, gw_ref, gb_ref, g_ref, o_ref, hr_ref, hi_ref, cr_ref, ci_ref,
                *, ts):
    i = pl.program_id(0)

    @pl.when(i == 0)
    def _():
        cr_ref[...] = jnp.zeros_like(cr_ref)
        ci_ref[...] = jnp.zeros_like(ci_ref)

    u = u_ref[...]
    ub = u.astype(BF16)
    hr_ref[...] = _dot(ub, wbr_ref[...])
    hi_ref[...] = _dot(ub, wbi_ref[...])
    groups = (ts // SSM_ROWS, SSM_ROWS, SSM_N)
    d = 1
    k = 0
    while d < SSM_ROWS:
        hr = hr_ref[...].reshape(groups)
        hi = hi_ref[...].reshape(groups)
        sr = pltpu.roll(hr, d, 1)
        si = pltpu.roll(hi, d, 1)
        ar = stepr_ref[k]
        ai = stepi_ref[k]
        hr_ref[...] = (hr + (ar * sr - ai * si)).reshape(ts, SSM_N)
        hi_ref[...] = (hi + (ar * si + ai * sr)).reshape(ts, SSM_N)
        d *= 2
        k += 1
    cr = cr_ref[...]
    ci = ci_ref[...]
    pr = powr_ref[...]
    pi = powi_ref[...]
    for r in range(ts // SSM_ROWS):
        sl = slice(r * SSM_ROWS, (r + 1) * SSM_ROWS)
        cbr = jnp.broadcast_to(cr, (SSM_ROWS, SSM_N))
        cbi = jnp.broadcast_to(ci, (SSM_ROWS, SSM_N))
        hr = hr_ref[sl, :] + (pr * cbr - pi * cbi)
        hi = hi_ref[sl, :] + (pr * cbi + pi * cbr)
        hr_ref[sl, :] = hr
        hi_ref[sl, :] = hi
        cr = hr[SSM_ROWS - 1:SSM_ROWS, :]
        ci = hi[SSM_ROWS - 1:SSM_ROWS, :]
    cr_ref[...] = cr
    ci_ref[...] = ci
    y = (_dot(hr_ref[...].astype(BF16), wcr_ref[...])
         - _dot(hi_ref[...].astype(BF16), wci_ref[...]))
    y = y + d_ref[...] * u
    gate = _sigmoid(_dot(y.astype(BF16), gw_ref[...].astype(BF16)) + gb_ref[...])
    o_ref[...] = _rms(_gelu_tanh(y) * gate, g_ref[...]).astype(BF16)


def ssm_mixer(proj, lam_re, lam_im, log_dt, b_re, b_im, c_re, c_im, d_skip, glu_w, glu_b, g,
              ts=256):
    S = proj.shape[0]
    NG, P, GC = N_SSM_GROUPS, SSM_STATE, SSM_GROUP
    lr = jnp.minimum(lam_re, -1e-4)
    li = lam_im
    dt = jnp.exp(log_dt)[:, None]
    mag = jnp.exp(lr * dt)
    ar = mag * jnp.cos(li * dt)
    ai = mag * jnp.sin(li * dt)
    den = lr * lr + li * li
    cr = ((ar - 1.0) * lr + ai * li) / den
    ci = (ai * lr - (ar - 1.0) * li) / den
    bbr = cr[..., None] * b_re - ci[..., None] * b_im
    bbi = cr[..., None] * b_im + ci[..., None] * b_re
    eye = jnp.eye(NG, dtype=F32)
    wb = lambda b: jnp.einsum('gpc,gh->gchp', b, eye).reshape(NG * GC, NG * P).astype(BF16)
    wc = lambda c: jnp.einsum('gcp,gh->gphc', c, eye).reshape(NG * P, NG * GC).astype(BF16)

    def powers(n):
        n = n.astype(F32)[:, None]
        lrd = (lr * dt).reshape(1, NG * P)
        lid = (li * dt).reshape(1, NG * P)
        m = jnp.exp(n * lrd)
        return m * jnp.cos(n * lid), m * jnp.sin(n * lid)

    nsteps = int(math.log2(SSM_ROWS))
    dist = 2 ** jnp.arange(nsteps)
    stepr, stepi = powers(dist)
    reach = (jnp.arange(SSM_ROWS)[None, :] >= dist[:, None]).astype(F32)[:, :, None]
    stepr = stepr[:, None, :] * reach
    stepi = stepi[:, None, :] * reach
    powr, powi = powers(jnp.arange(1, SSM_ROWS + 1))
    row = lambda v: v.reshape(1, -1)
    full = lambda a: pl.BlockSpec(a.shape, lambda i: (0,) * a.ndim)
    args = [wb(bbr), wb(bbi), wc(c_re), wc(c_im), stepr, stepi, powr, powi, row(d_skip), glu_w,
            row(glu_b), row(g)]
    return pl.pallas_call(
        functools.partial(_ssm_kernel, ts=ts),
        grid=(S // ts,),
        in_specs=[pl.BlockSpec((ts, D_SSM), lambda i: (i, (D_MAIN - D_SSM) // D_SSM))]
                 + [full(a) for a in args],
        out_specs=pl.BlockSpec((ts, D_SSM), lambda i: (i, 0)),
        out_shape=jax.ShapeDtypeStruct((S, D_SSM), BF16),
        scratch_shapes=[pltpu.VMEM((ts, SSM_N), F32), pltpu.VMEM((ts, SSM_N), F32),
                        pltpu.VMEM((1, SSM_N), F32), pltpu.VMEM((1, SSM_N), F32)],
        compiler_params=_cparams(("arbitrary",), VMEM_LIMIT),
        name="ssm_mixer",
    )(proj, *args)


FFN_TM = 1024
FFN_TF = 256
FFN_SUB = 512


def _swiglu_kernel(te_ref, nv_ref, x_ref, g_ref, wg_ref, wu_ref, wd_ref, o_ref, xn_ref, *, tm):
    i = pl.program_id(0)
    j = pl.program_id(1)
    nvalid = nv_ref[i]
    subs = [slice(sb * FFN_SUB, (sb + 1) * FFN_SUB) for sb in range(tm // FFN_SUB)]

    def weights():
        return wg_ref[0].astype(BF16), wu_ref[0].astype(BF16), wd_ref[0].astype(BF16)

    def normalise(sl):
        xn_ref[sl, :] = _rms(x_ref[sl, :], g_ref[...]).astype(BF16)

    def expert(sl, w):
        x = xn_ref[sl, :]
        a = (_silu(_dot(x, w[0])) * _dot(x, w[1])).astype(BF16)
        return _dot(a, w[2])

    dense = nvalid > tm - FFN_SUB

    @pl.when(dense & (j == 0))
    def _():
        w = weights()
        for sl in subs:
            normalise(sl)
            o_ref[sl, :] = expert(sl, w)

    @pl.when(dense & (j > 0))
    def _():
        w = weights()
        for sl in subs:
            o_ref[sl, :] += expert(sl, w)

    @pl.when(jnp.logical_not(dense))
    def _():
        w = weights()
        for sb, sl in enumerate(subs):
            @pl.when((sb * FFN_SUB < nvalid) & (j == 0))
            def _():
                normalise(sl)
                o_ref[sl, :] = expert(sl, w)

            @pl.when((sb * FFN_SUB < nvalid) & (j > 0))
            def _():
                o_ref[sl, :] += expert(sl, w)

            @pl.when((sb * FFN_SUB >= nvalid) & (j == 0))
            def _():
                o_ref[sl, :] = jnp.zeros((FFN_SUB, o_ref.shape[1]), F32)


def swiglu(x, g, wg, wu, wd, tile_expert, tile_valid, tm=FFN_TM, tf=FFN_TF):
    N, D = x.shape
    F = wg.shape[2]
    tm = min(tm, N)
    last = F // tf - 1

    def wj(i, j, nv):
        return jnp.where(nv[i] > 0, j, last)

    in_specs = [pl.BlockSpec((tm, D), lambda i, j, te, nv: (i, 0)),
                pl.BlockSpec((1, D), lambda i, j, te, nv: (0, 0)),
                pl.BlockSpec((1, D, tf), lambda i, j, te, nv: (te[i], 0, wj(i, j, nv))),
                pl.BlockSpec((1, D, tf), lambda i, j, te, nv: (te[i], 0, wj(i, j, nv))),
                pl.BlockSpec((1, tf, D), lambda i, j, te, nv: (te[i], wj(i, j, nv), 0))]
    return pl.pallas_call(
        functools.partial(_swiglu_kernel, tm=tm),
        grid_spec=pltpu.PrefetchScalarGridSpec(
            num_scalar_prefetch=2,
            grid=(N // tm, F // tf),
            in_specs=in_specs,
            out_specs=pl.BlockSpec((tm, D), lambda i, j, te, nv: (i, 0)),
            scratch_shapes=[pltpu.VMEM((tm, D), BF16)]),
        out_shape=jax.ShapeDtypeStruct((N, D), F32),
        compiler_params=_cparams(("arbitrary", "arbitrary"), VMEM_LIMIT),
        name="swiglu",
    )(tile_expert, tile_valid, x, g.reshape(1, D), wg, wu, wd)


def _route_kernel(lg_ref, info_ref, cnt_ref, run_ref, *, tb):
    i = pl.program_id(0)

    @pl.when(i == 0)
    def _():
        run_ref[...] = jnp.zeros_like(run_ref)

    lane = lax.broadcasted_iota(jnp.int32, (tb, LANES), 1)
    lg = jnp.where(lane < N_EXPERTS, lg_ref[...], -jnp.inf)
    m1 = jnp.max(lg, axis=1, keepdims=True)
    i1 = jnp.min(jnp.where(lg == m1, lane, LANES), axis=1, keepdims=True)
    lg2 = jnp.where(lane == i1, -jnp.inf, lg)
    m2 = jnp.max(lg2, axis=1, keepdims=True)
    i2 = jnp.min(jnp.where(lg2 == m2, lane, LANES), axis=1, keepdims=True)
    e2 = jnp.exp(m2 - m1)
    inv = 1.0 / (1.0 + e2)
    w1 = inv
    w2 = e2 * inv
    onehot = jnp.where((lane == i1) | (lane == i2), 1.0, 0.0)
    r = lax.broadcasted_iota(jnp.int32, (tb, tb), 0)
    c = lax.broadcasted_iota(jnp.int32, (tb, tb), 1)
    tri = jnp.where(c < r, 1.0, 0.0).astype(BF16)
    rank = _dot(tri, onehot.astype(BF16)) + run_ref[...]
    r1 = jnp.sum(jnp.where(lane == i1, rank, 0.0), axis=1, keepdims=True)
    r2 = jnp.sum(jnp.where(lane == i2, rank, 0.0), axis=1, keepdims=True)
    info = jnp.where(lane == 0, i1.astype(F32), 0.0)
    info = jnp.where(lane == 1, i2.astype(F32), info)
    info = jnp.where(lane == 2, r1, info)
    info = jnp.where(lane == 3, r2, info)
    info = jnp.where(lane == 4, w1, info)
    info = jnp.where(lane == 5, w2, info)
    info_ref[...] = info
    run_ref[...] = run_ref[...] + jnp.sum(onehot, axis=0, keepdims=True)
    cnt_ref[...] = run_ref[...]


def moe_route(logits, tb=512):
    S = logits.shape[0]
    return pl.pallas_call(
        functools.partial(_route_kernel, tb=tb),
        grid=(S // tb,),
        in_specs=[pl.BlockSpec((tb, LANES), lambda i: (i, 0))],
        out_specs=[pl.BlockSpec((tb, LANES), lambda i: (i, 0)),
                   pl.BlockSpec((1, LANES), lambda i: (0, 0))],
        out_shape=[jax.ShapeDtypeStruct((S, LANES), F32), jax.ShapeDtypeStruct((1, LANES), F32)],
        scratch_shapes=[pltpu.VMEM((1, LANES), F32)],
        compiler_params=_cparams(("arbitrary",)),
        name="moe_route",
    )(logits)


DISPATCH_TB = 256
ZERO_ROWS = 128
ZERO_REGIONS = N_EXPERTS + 1


def _dispatch_kernel(pos_ref, fill_ref, x_ref, o_ref, zero_ref, sem, zsem, *, tb, s):
    i = pl.program_id(0)

    def zero_region(first, n, wait):
        def piece(off, rows, aligned):
            if aligned:
                off = pl.multiple_of(off, SUBLANES)
            cp = pltpu.make_async_copy(zero_ref.at[pl.ds(0, rows), :],
                                       o_ref.at[pl.ds(off, rows), :], zsem)
            cp.wait() if wait else cp.start()

        head = jnp.minimum((-first) & (SUBLANES - 1), n)
        lax.fori_loop(0, head, lambda r, c: (piece(first + r, 1, False), c)[1], 0)
        base = first + head
        m = n - head
        big = ZERO_ROWS
        lax.fori_loop(0, m // big, lambda r, c: (piece(base + r * big, big, True), c)[1], 0)
        rem = m % big
        size = big // 2
        while size >= SUBLANES:
            @pl.when((rem & size) != 0)
            def _():
                piece(base + (m - rem) + (rem & ~(2 * size - 1)), size, True)
            size //= 2
        tail0 = base + (m & ~(SUBLANES - 1))
        lax.fori_loop(0, m & (SUBLANES - 1), lambda r, c: (piece(tail0 + r, 1, False), c)[1], 0)

    @pl.when(i == 0)
    def _():
        zero_ref[...] = jnp.zeros_like(zero_ref)
        for wait in (False, True):
            for g in range(ZERO_REGIONS):
                zero_region(fill_ref[g], fill_ref[ZERO_REGIONS + g], wait)

    def copy(r, k):
        dst = pos_ref[k * s + i * tb + r]
        return pltpu.make_async_copy(x_ref.at[pl.ds(r, 1), :], o_ref.at[pl.ds(dst, 1), :], sem)

    def start(r, carry):
        copy(r, 0).start(priority=0)
        copy(r, 1).start(priority=1)
        return carry

    lax.fori_loop(0, tb, start, 0)

    def drain(r, carry):
        copy(r, 0).wait()
        copy(r, 1).wait()
        return carry

    lax.fori_loop(0, tb, drain, 0)


def moe_dispatch(x, pos, fill, n_rows, tb=DISPATCH_TB):
    S, D = x.shape
    return pl.pallas_call(
        functools.partial(_dispatch_kernel, tb=tb, s=S),
        grid_spec=pltpu.PrefetchScalarGridSpec(
            num_scalar_prefetch=2,
            grid=(S // tb,),
            in_specs=[pl.BlockSpec((tb, D), lambda i, pos, fill: (i, 0))],
            out_specs=pl.BlockSpec(memory_space=pl.ANY),
            scratch_shapes=[pltpu.VMEM((ZERO_ROWS, D), x.dtype),
                            pltpu.SemaphoreType.DMA(()),
                            pltpu.SemaphoreType.DMA(())]),
        out_shape=jax.ShapeDtypeStruct((n_rows, D), x.dtype),
        compiler_params=_cparams(("arbitrary",)),
        name="moe_dispatch",
    )(pos, fill, x)


COMBINE_TB = 256


def _combine_kernel(pos_ref, h_ref, info_ref, g_ref, y_ref, o_ref, buf_ref, sem, *, tb, s):
    i = pl.program_id(0)

    def copy(r, k):
        src = pos_ref[k * s + i * tb + r]
        return pltpu.make_async_copy(y_ref.at[pl.ds(src, 1), :], buf_ref.at[k, pl.ds(r, 1), :], sem)

    def start(r, carry):
        copy(r, 0).start(priority=0)
        copy(r, 1).start(priority=1)
        return carry

    lax.fori_loop(0, tb, start, 0)

    def drain(r, carry):
        copy(r, 0).wait()
        copy(r, 1).wait()
        return carry

    lax.fori_loop(0, tb, drain, 0)
    w1 = info_ref[:, 4:5]
    w2 = info_ref[:, 5:6]
    first_lower = info_ref[:, 0:1] < info_ref[:, 1:2]
    ya = jnp.where(first_lower, w1 * buf_ref[0], w2 * buf_ref[1])
    yb = jnp.where(first_lower, w2 * buf_ref[1], w1 * buf_ref[0])
    o_ref[...] = _rms(h_ref[...] + (ya + yb), g_ref[...])


def moe_combine(h, info, pos, y_sorted, g, tb=COMBINE_TB):
    S, D = h.shape
    return pl.pallas_call(
        functools.partial(_combine_kernel, tb=tb, s=S),
        grid_spec=pltpu.PrefetchScalarGridSpec(
            num_scalar_prefetch=1,
            grid=(S // tb,),
            in_specs=[pl.BlockSpec((tb, D), lambda i, pos: (i, 0)),
                      pl.BlockSpec((tb, LANES), lambda i, pos: (i, 0)),
                      pl.BlockSpec((1, D), lambda i, pos: (0, 0)),
                      pl.BlockSpec(memory_space=pl.ANY)],
            out_specs=pl.BlockSpec((tb, D), lambda i, pos: (i, 0)),
            scratch_shapes=[pltpu.VMEM((2, tb, D), F32), pltpu.SemaphoreType.DMA(())]),
        out_shape=jax.ShapeDtypeStruct((S, D), F32),
        compiler_params=_cparams(("arbitrary",), VMEM_LIMIT),
        name="moe_combine",
    )(pos, h, info, g.reshape(1, D), y_sorted)


def moe_layer(h, norm_g, logits, wg, wu, wd, final_g, tm=FFN_TM):
    S, D = h.shape
    E = N_EXPERTS
    tm = min(tm, S)
    info, counts = moe_route(logits)
    cnt = counts[0, :E].astype(jnp.int32)
    padded = ((cnt + tm - 1) // tm) * tm
    ends = jnp.cumsum(padded)
    offs = ends - padded
    n_tiles = (2 * S) // tm + E
    e12 = info[:, 0:2].astype(jnp.int32)
    r12 = info[:, 2:4].astype(jnp.int32)
    pos = (jnp.take(offs, e12) + r12).T.reshape(2 * S)
    tile_start = jnp.arange(n_tiles, dtype=jnp.int32) * tm
    tile_expert = jnp.minimum(jnp.sum(tile_start[:, None] >= ends[None, :], axis=1), E - 1)
    tile_expert = tile_expert.astype(jnp.int32)
    tile_valid = jnp.clip(jnp.take(offs + cnt, tile_expert) - tile_start, 0, tm).astype(jnp.int32)
    n_rows = n_tiles * tm
    fill = jnp.concatenate([offs + cnt, ends[-1:], padded - cnt, n_rows - ends[-1:]])
    fill = fill.astype(jnp.int32)
    xs = moe_dispatch(h, pos, fill, n_tiles * tm)
    ys = swiglu(xs, norm_g, wg, wu, wd, tile_expert, tile_valid, tm=tm)
    return moe_combine(h, info, pos, ys, final_g)


def _final_norm_kernel(x_ref, g_ref, o_ref):
    o_ref[...] = _rms(x_ref[...], g_ref[...])


def final_norm(x, g, tm=512):
    S, D = x.shape
    return pl.pallas_call(
        _final_norm_kernel,
        grid=(S // tm,),
        in_specs=[pl.BlockSpec((tm, D), lambda i: (i, 0)), pl.BlockSpec((1, D), lambda i: (0, 0))],
        out_specs=pl.BlockSpec((tm, D), lambda i: (i, 0)),
        out_shape=jax.ShapeDtypeStruct((S, D), F32),
        compiler_params=_cparams(("arbitrary",), VMEM_LIMIT),
        name="final_norm",
    )(x, g.reshape(1, D))


def _pad_cols(w, n):
    return jnp.concatenate([w, jnp.zeros((w.shape[0], n - w.shape[1]), w.dtype)], axis=1)


def kernel(x, positions, norm_mix_g, w_in, conv_dw_w, conv_dw_b, conv_ln_g, conv_ln_b, conv_pw_w, conv_pw_b, nsa_gate_b, nsa_pe_k, nsa_pe_v, nsa_w1k, nsa_w2k, nsa_w1v, nsa_w2v, ssm_lambda_re, ssm_lambda_im, ssm_log_dt, ssm_b_re, ssm_b_im, ssm_c_re, ssm_c_im, ssm_d, ssm_glu_w, ssm_glu_b, mix_out_g, w_out, norm_ffn_g, ffn_w_gate, ffn_w_up, ffn_w_down, router_w, moe_w_gate, moe_w_up, moe_w_down, final_norm_g):
    B, S, D = x.shape
    depth = w_in.shape[0]
    outs = []
    for b in range(B):
        h = x[b]
        pos = positions[b]
        pending = None
        out = None
        for l in range(depth):
            w_main = jnp.concatenate([w_in[l][:, :OFF_GATE], w_in[l][:, OFF_SSM:]], axis=1)
            w_gate = _pad_cols(w_in[l][:, OFF_GATE:OFF_SSM], LANES)
            if pending is None:
                hn, gate_logits = norm_side(h, norm_mix_g[l], w_gate, exact_side=False)
            else:
                h, hn, gate_logits = norm_side(h, norm_mix_g[l], w_gate, exact_side=False,
                                               add=pending)
                pending = None
            proj = matmul(hn, w_main)
            g_mix = mix_out_g[l]
            conv_o = conv_module(proj, conv_dw_w[l], conv_dw_b[l], conv_ln_g[l], conv_ln_b[l],
                                 conv_pw_w[l], conv_pw_b[l], g_mix[:D_CONV])
            q_raw, q_rope_t, cv_raw, ks_aug, vs_t, kw, vw_t, gates = attn_prep(
                proj, gate_logits, nsa_gate_b[l], pos)
            kc, kct = nsa_compress(cv_raw, jnp.stack([nsa_w1k[l], nsa_w1v[l]]),
                                   jnp.stack([nsa_w2k[l], nsa_w2v[l]]),
                                   jnp.stack([nsa_pe_k[l], nsa_pe_v[l]]))
            o_c, mask = nsa_compressed(q_raw, kc, kct)
            ns = mask.shape[1]
            if ns % LANES:
                mask = jnp.pad(mask, ((0, 0), (0, LANES - ns % LANES), (0, 0)),
                               constant_values=1.0)
            o_s = nsa_selected(q_rope_t, mask, ks_aug, vs_t)
            o_w = nsa_window(q_rope_t, kw, vw_t)
            att_o = nsa_gate(o_c, o_s, o_w, gates, g_mix[D_CONV:D_CONV + D_ATT])
            ssm_o = ssm_mixer(proj, ssm_lambda_re[l], ssm_lambda_im[l], ssm_log_dt[l],
                              ssm_b_re[l], ssm_b_im[l], ssm_c_re[l], ssm_c_im[l], ssm_d[l],
                              ssm_glu_w[l], ssm_glu_b[l], g_mix[D_CONV + D_ATT:])
            mixed = jnp.concatenate([conv_o, att_o, ssm_o], axis=1)
            h = matmul(mixed, w_out[l], res=h)
            i = l // 2
            if l % 2 == 0:
                n_t = S // min(FFN_TM, S)
                pending = swiglu(h, norm_ffn_g[l], ffn_w_gate[i:i + 1], ffn_w_up[i:i + 1],
                                 ffn_w_down[i:i + 1], jnp.zeros((n_t,), jnp.int32),
                                 jnp.full((n_t,), min(FFN_TM, S), jnp.int32))
            else:
                if l != depth - 1:
                    raise NotImplementedError("a MoE layer is only supported as the last layer")
                (logits,) = norm_side(h, norm_ffn_g[l], _pad_cols(router_w[i], LANES),
                                      exact_side=True, emit_hn=False)
                out = moe_layer(h, norm_ffn_g[l], logits, moe_w_gate[i], moe_w_up[i],
                                moe_w_down[i], final_norm_g)
        if out is None:
            if pending is not None:
                h = h + pending
            out = final_norm(h, final_norm_g)
        outs.append(out)
    return jnp.stack(outs)
```
